```python
import math
import jax, jax.numpy as jnp
from jax import lax
import numpy as np

D_MODEL = 1024
BATCH = 4
SEQ = 4096
DEPTH = 1
DEC_BATCH = 32
DEC_SEQ = 1
PAST_LEN = 8192
PAGE_SIZE = 128

D_RNN = 1024
N_RNN_BLOCKS = 8
RNN_BLOCK = D_RNN // N_RNN_BLOCKS
CONV_WIDTH = 4
LRU_C = 8.0
GROUPS = ((128, 1), (512, 4), (2048, 16))
HEADS_PER_GROUP = 4
HEAD_DIM = 128
N_ATTN_HEADS = HEADS_PER_GROUP * len(GROUPS)
D_QKV = N_ATTN_HEADS * HEAD_DIM
D_ATTN_OUT = HEADS_PER_GROUP * HEAD_DIM
N_BUCKETS = 32
MAX_DISTANCE = 2048
QUERY_BLOCK = 128
D_IN = 2 * D_RNN + 3 * D_QKV + 2 * D_MODEL
N_EXPERTS = 32
TOP_K = 4
D_EXPERT = 1024
SWIGLU_ALPHA = 1.702
SWIGLU_LIMIT = 7.0
EXPERT_BLOCK = 128
D_PLE = 256
EPS = 1e-6

kernel_name = 'hawk_dilated_moe_decoder_step'


def rmsnorm(x, g):
    xf = x.astype(jnp.float32)
    y = xf * lax.rsqrt(jnp.mean(xf * xf, axis=-1, keepdims=True) + EPS)
    return (y * g.astype(jnp.float32)).astype(x.dtype)


def t5_causal_buckets(dist):
    max_exact = N_BUCKETS // 2
    d = np.maximum(dist, 1).astype(np.float32)
    large = max_exact + (np.log(d / max_exact) / np.log(MAX_DISTANCE / max_exact)
                         * (N_BUCKETS - max_exact)).astype(np.int32)
    large = np.minimum(large, N_BUCKETS - 1)
    return np.where(dist < max_exact, dist, large).astype(np.int32)


def group_bias(rel_bias, gi, window, dilation):
    buckets = t5_causal_buckets(np.arange(window // dilation + 1) * dilation)
    heads = slice(gi * HEADS_PER_GROUP, (gi + 1) * HEADS_PER_GROUP)
    return rel_bias[buckets][:, heads].T.astype(jnp.float32)


def causal_conv(xr, buf, w, b):
    T = xr.shape[1]
    xe = jnp.concatenate([buf, xr.astype(buf.dtype)], axis=1)
    y = sum(xe[:, j:j + T] * w[j] for j in range(CONV_WIDTH)) + b
    return y, xe[:, -(CONV_WIDTH - 1):]


def rg_lru(xc, h0, w_a, b_a, w_x, b_x, lam):
    B, T, _ = xc.shape
    xf = xc.astype(jnp.float32)
    xb = xf.reshape(B, T, N_RNN_BLOCKS, RNN_BLOCK)
    r = jax.nn.sigmoid(jnp.einsum('btnc,ncd->btnd', xb, w_a.astype(jnp.float32)) + b_a).reshape(B, T, D_RNN)
    i = jax.nn.sigmoid(jnp.einsum('btnc,ncd->btnd', xb, w_x.astype(jnp.float32)) + b_x).reshape(B, T, D_RNN)
    log_a = -LRU_C * r * jax.nn.softplus(-lam.astype(jnp.float32))
    a = jnp.exp(log_a)
    bterm = jnp.sqrt(-jnp.expm1(2.0 * log_a)) * (i * xf)
    bterm = bterm.at[:, 0].add(a[:, 0] * h0.astype(jnp.float32))

    def combine(left, right):
        a1, b1 = left
        a2, b2 = right
        return a1 * a2, a2 * b1 + b2

    _, h = lax.associative_scan(combine, (a, bterm), axis=1)
    return h, h[:, -1].astype(h0.dtype)


def dilated_group_attention(q, k_ctx, v_ctx, prefix_len, dilation, window, bias):
    B, T, H, hd = q.shape
    nk = window // dilation + 1
    qb = math.gcd(T, QUERY_BLOCK)
    nb = T // qb
    offs = jnp.arange(nk, dtype=jnp.int32) * dilation
    scale = HEAD_DIM ** -0.5
    neg = jnp.finfo(jnp.float32).min

    def block(bi):
        t0 = bi * qb
        qblk = lax.dynamic_slice_in_dim(q, t0, qb, axis=1)
        idx = prefix_len + t0 + jnp.arange(qb, dtype=jnp.int32)[:, None] - offs[None, :]
        valid = idx >= 0
        idx = jnp.maximum(idx, 0)
        kg = k_ctx[:, idx]
        vg = v_ctx[:, idx].astype(jnp.float32)
        s = jnp.einsum('bqhd,bqjhd->bhqj', qblk, kg, preferred_element_type=jnp.float32) * scale
        s = jnp.where(valid[None, None], s + bias[None, :, None, :], neg)
        m = jnp.max(s, axis=-1, keepdims=True)
        e = jnp.exp(s - m)
        den = jnp.sum(e, axis=-1, keepdims=True)
        o = jnp.einsum('bhqj,bqjhd->bqhd', e / den, vg)
        lse = (m + jnp.log(den))[..., 0]
        return o, jnp.transpose(lse, (0, 2, 1))

    o, lse = lax.map(block, jnp.arange(nb, dtype=jnp.int32))
    o = jnp.moveaxis(o, 0, 1).reshape(B, T, H, hd)
    lse = jnp.moveaxis(lse, 0, 1).reshape(B, T, H)
    return o, lse


def moe_ffn(xn, w_router, b_router, w_gate_up, b_gate_up, w_down, b_down):
    B, T, D = xn.shape
    xf = xn.reshape(-1, D)
    n_tok = xf.shape[0]
    logits = (xf @ w_router).astype(jnp.float32) + b_router.astype(jnp.float32)
    top_v, top_e = lax.top_k(logits, TOP_K)
    gates = jax.nn.softmax(top_v, axis=-1)
    n_assign = n_tok * TOP_K
    flat_e = top_e.reshape(-1).astype(jnp.int32)
    order = jnp.argsort(flat_e, stable=True)
    e_sorted = flat_e[order]
    tok_sorted = (order // TOP_K).astype(jnp.int32)
    g_sorted = gates.reshape(-1)[order]
    counts = jnp.bincount(flat_e, length=N_EXPERTS).astype(jnp.int32)
    padded = (counts + EXPERT_BLOCK - 1) // EXPERT_BLOCK * EXPERT_BLOCK
    pad_end = jnp.cumsum(padded)
    pad_start = pad_end - padded
    start = jnp.cumsum(counts) - counts
    dest = pad_start[e_sorted] + jnp.arange(n_assign, dtype=jnp.int32) - start[e_sorted]
    n_blocks = -(-n_assign // EXPERT_BLOCK) + N_EXPERTS
    rows = n_blocks * EXPERT_BLOCK
    row_tok = jnp.full((rows,), n_tok, jnp.int32).at[dest].set(tok_sorted)
    block_e = jnp.minimum(jnp.searchsorted(pad_end, jnp.arange(n_blocks, dtype=jnp.int32) * EXPERT_BLOCK,
                                           side='right'), N_EXPERTS - 1).astype(jnp.int32)
    x_pad = jnp.concatenate([xf, jnp.zeros((1, D), xf.dtype)], axis=0)
    xb = x_pad[row_tok].reshape(n_blocks, EXPERT_BLOCK, D)

    def expert_block(args):
        xblk, e = args
        gu = xblk @ w_gate_up[e] + b_gate_up[e]
        g = jnp.minimum(gu[:, :D_EXPERT], SWIGLU_LIMIT)
        u = jnp.clip(gu[:, D_EXPERT:], -SWIGLU_LIMIT, SWIGLU_LIMIT)
        h = (u + 1.0) * (g * jax.nn.sigmoid(SWIGLU_ALPHA * g))
        return h @ w_down[e] + b_down[e]

    yb = lax.map(expert_block, (xb, block_e)).reshape(rows, D)
    y_assign = yb[dest].astype(jnp.float32) * g_sorted[:, None]
    y = jnp.zeros((n_tok, D), jnp.float32).at[tok_sorted].add(y_assign)
    return y.reshape(B, T, D).astype(xn.dtype)


def token_mixers(u, conv_buf, h0, kv_bufs, w_in, conv_w, conv_b, w_rg_a, b_rg_a, w_rg_x, b_rg_x,
                 lru_lambda, q_norm_g, k_norm_g, rel_bias, w_br_rnn, w_br_attn, w_out):
    B, T, _ = u.shape
    z = u @ w_in
    s1 = D_RNN
    s2 = s1 + D_RNN
    s3 = s2 + D_QKV
    s4 = s3 + D_QKV
    s5 = s4 + D_QKV
    s6 = s5 + D_MODEL
    xr, yg, q, k, v, gate_a, gate_b = jnp.split(z, [s1, s2, s3, s4, s5, s6], axis=-1)
    xc, new_conv = causal_conv(xr, conv_buf, conv_w, conv_b)
    h, h_last = rg_lru(xc, h0, w_rg_a, b_rg_a, w_rg_x, b_rg_x, lru_lambda)
    a_out = (h * jax.nn.gelu(yg.astype(jnp.float32))).astype(u.dtype)
    q = rmsnorm(q.reshape(B, T, N_ATTN_HEADS, HEAD_DIM), q_norm_g)
    k = rmsnorm(k.reshape(B, T, N_ATTN_HEADS, HEAD_DIM), k_norm_g)
    v = v.reshape(B, T, N_ATTN_HEADS, HEAD_DIM)
    outs, lses, new_kv = [], [], []
    for gi, (window, dilation) in enumerate(GROUPS):
        hs = slice(gi * HEADS_PER_GROUP, (gi + 1) * HEADS_PER_GROUP)
        qg, kg, vg = q[:, :, hs], k[:, :, hs], v[:, :, hs]
        buf = kv_bufs[gi]
        k_ctx = jnp.concatenate([buf[:, :, 0], kg.astype(buf.dtype)], axis=1)
        v_ctx = jnp.concatenate([buf[:, :, 1], vg.astype(buf.dtype)], axis=1)
        o, lse = dilated_group_attention(qg, k_ctx, v_ctx, buf.shape[1], dilation, window,
                                         group_bias(rel_bias, gi, window, dilation))
        outs.append(o)
        lses.append(lse)
        new_kv.append(jnp.stack([kg, vg], axis=2).astype(buf.dtype))
    wts = jax.nn.softmax(jnp.stack(lses, axis=0), axis=0)
    b_out = jnp.einsum('gbth,gbthd->bthd', wts, jnp.stack(outs, axis=0)).reshape(B, T, D_ATTN_OUT).astype(u.dtype)
    merged = jax.nn.sigmoid(gate_a) * (a_out @ w_br_rnn) + jax.nn.sigmoid(gate_b) * (b_out @ w_br_attn)
    return merged @ w_out, new_conv, h_last, new_kv


def decoder_layer(x, p, conv_buf, h0, kv_bufs, norm_mix_g, w_in, conv_w, conv_b, w_rg_a, b_rg_a,
                  w_rg_x, b_rg_x, lru_lambda, q_norm_g, k_norm_g, rel_bias, w_br_rnn, w_br_attn, w_out,
                  norm_ffn_g, w_router, b_router, w_gate_up, b_gate_up, w_down, b_down,
                  norm_ple_g, w_ple_gate, w_ple_proj):
    u = rmsnorm(x, norm_mix_g)
    mix, new_conv, h_last, new_kv = token_mixers(u, conv_buf, h0, kv_bufs, w_in, conv_w, conv_b,
                                                 w_rg_a, b_rg_a, w_rg_x, b_rg_x, lru_lambda,
                                                 q_norm_g, k_norm_g, rel_bias, w_br_rnn, w_br_attn, w_out)
    x = x + mix
    x = x + moe_ffn(rmsnorm(x, norm_ffn_g), w_router, b_router, w_gate_up, b_gate_up, w_down, b_down)
    gate = jax.nn.sigmoid(rmsnorm(x, norm_ple_g) @ w_ple_gate)
    x = x + gate * (p.astype(x.dtype) @ w_ple_proj)
    return x, new_conv, h_last, new_kv


def setup_inputs(seed: int = 0) -> dict:
    key = jax.random.key(seed)
    ks = iter(jax.random.split(key, 48))
    f32 = jnp.float32

    def nrm(shape, scale):
        return jax.random.normal(next(ks), shape, f32) * scale

    lens = [min(w, PAST_LEN) for w, _ in GROUPS]
    kvs = (HEADS_PER_GROUP, HEAD_DIM)
    x_prompt = nrm((BATCH, SEQ, D_MODEL), 1.0)
    x_sample = nrm((DEC_BATCH, DEC_SEQ, D_MODEL), 1.0)
    state_conv = nrm((DEPTH, DEC_BATCH, CONV_WIDTH - 1, D_RNN), 1.0)
    state_h = nrm((DEPTH, DEC_BATCH, D_RNN), 0.5)
    cache_kv_w128 = nrm((DEPTH, DEC_BATCH, lens[0], 2) + kvs, 1.0)
    cache_kv_w512 = nrm((DEPTH, DEC_BATCH, lens[1], 2) + kvs, 1.0)
    cache_kv_w2048 = nrm((DEPTH, DEC_BATCH, lens[2], 2) + kvs, 1.0)
    p_prompt = nrm((DEPTH, BATCH, SEQ, D_PLE), 1.0)
    p_sample = nrm((DEPTH, DEC_BATCH, DEC_SEQ, D_PLE), 1.0)
    norm_mix_g = 1.0 + nrm((DEPTH, D_MODEL), 0.05)
    w_in = nrm((DEPTH, D_MODEL, D_IN), D_MODEL ** -0.5)
    conv_w = nrm((DEPTH, CONV_WIDTH, D_RNN), CONV_WIDTH ** -0.5)
    conv_b = nrm((DEPTH, D_RNN), 0.01)
    w_rg_a = nrm((DEPTH, N_RNN_BLOCKS, RNN_BLOCK, RNN_BLOCK), RNN_BLOCK ** -0.5)
    b_rg_a = nrm((DEPTH, N_RNN_BLOCKS, RNN_BLOCK), 0.01)
    w_rg_x = nrm((DEPTH, N_RNN_BLOCKS, RNN_BLOCK, RNN_BLOCK), RNN_BLOCK ** -0.5)
    b_rg_x = nrm((DEPTH, N_RNN_BLOCKS, RNN_BLOCK), 0.01)
    a0 = jax.random.uniform(next(ks), (DEPTH, D_RNN), f32, 0.9, 0.999)
    s = a0 ** (1.0 / LRU_C)
    lru_lambda = jnp.log(s) - jnp.log1p(-s)
    q_norm_g = 1.0 + nrm((DEPTH, HEAD_DIM), 0.05)
    k_norm_g = 1.0 + nrm((DEPTH, HEAD_DIM), 0.05)
    rel_bias = nrm((N_BUCKETS, N_ATTN_HEADS), 0.5)
    w_br_rnn = nrm((DEPTH, D_RNN, D_MODEL), D_RNN ** -0.5)
    w_br_attn = nrm((DEPTH, D_ATTN_OUT, D_MODEL), D_ATTN_OUT ** -0.5)
    w_out = nrm((DEPTH, D_MODEL, D_MODEL), D_MODEL ** -0.5)
    norm_ffn_g = 1.0 + nrm((DEPTH, D_MODEL), 0.05)
    w_router = nrm((DEPTH, D_MODEL, N_EXPERTS), D_MODEL ** -0.5)
    b_router = nrm((DEPTH, N_EXPERTS), 0.01)
    w_gate_up = nrm((DEPTH, N_EXPERTS, D_MODEL, 2 * D_EXPERT), D_MODEL ** -0.5)
    b_gate_up = nrm((DEPTH, N_EXPERTS, 2 * D_EXPERT), 0.01)
    w_down = nrm((DEPTH, N_EXPERTS, D_EXPERT, D_MODEL), D_EXPERT ** -0.5)
    b_down = nrm((DEPTH, N_EXPERTS, D_MODEL), 0.01)
    norm_ple_g = 1.0 + nrm((DEPTH, D_MODEL), 0.05)
    w_ple_gate = nrm((DEPTH, D_MODEL, D_MODEL), D_MODEL ** -0.5)
    w_ple_proj = nrm((DEPTH, D_PLE, D_MODEL), D_PLE ** -0.5)
    return {'x_prompt': x_prompt, 'x_sample': x_sample, 'state_conv': state_conv, 'state_h': state_h,
            'cache_kv_w128': cache_kv_w128, 'cache_kv_w512': cache_kv_w512, 'cache_kv_w2048': cache_kv_w2048,
            'p_prompt': p_prompt, 'p_sample': p_sample, 'norm_mix_g': norm_mix_g, 'w_in': w_in,
            'conv_w': conv_w, 'conv_b': conv_b, 'w_rg_a': w_rg_a, 'b_rg_a': b_rg_a, 'w_rg_x': w_rg_x,
            'b_rg_x': b_rg_x, 'lru_lambda': lru_lambda, 'q_norm_g': q_norm_g, 'k_norm_g': k_norm_g,
            'rel_bias': rel_bias, 'w_br_rnn': w_br_rnn, 'w_br_attn': w_br_attn, 'w_out': w_out,
            'norm_ffn_g': norm_ffn_g, 'w_router': w_router, 'b_router': b_router, 'w_gate_up': w_gate_up,
            'b_gate_up': b_gate_up, 'w_down': w_down, 'b_down': b_down, 'norm_ple_g': norm_ple_g,
            'w_ple_gate': w_ple_gate, 'w_ple_proj': w_ple_proj}


def reference(x_prompt, x_sample, state_conv, state_h, cache_kv_w128, cache_kv_w512, cache_kv_w2048,
              p_prompt, p_sample, norm_mix_g, w_in, conv_w, conv_b, w_rg_a, b_rg_a, w_rg_x, b_rg_x,
              lru_lambda, q_norm_g, k_norm_g, rel_bias, w_br_rnn, w_br_attn, w_out, norm_ffn_g,
              w_router, b_router, w_gate_up, b_gate_up, w_down, b_down, norm_ple_g, w_ple_gate, w_ple_proj):
    dt = x_prompt.dtype
    bp = x_prompt.shape[0]
    y_prompt, y_sample = x_prompt, x_sample
    conv_p, conv_s, h_p, h_s = [], [], [], []
    kv_p = [[] for _ in GROUPS]
    kv_s = [[] for _ in GROUPS]
    for i in range(DEPTH):
        lw = (norm_mix_g[i], w_in[i], conv_w[i], conv_b[i], w_rg_a[i], b_rg_a[i], w_rg_x[i], b_rg_x[i],
              lru_lambda[i], q_norm_g[i], k_norm_g[i], rel_bias, w_br_rnn[i], w_br_attn[i], w_out[i],
              norm_ffn_g[i], w_router[i], b_router[i], w_gate_up[i], b_gate_up[i], w_down[i], b_down[i],
              norm_ple_g[i], w_ple_gate[i], w_ple_proj[i])
        empty_kv = tuple(jnp.zeros((bp, 0, 2, HEADS_PER_GROUP, HEAD_DIM), dt) for _ in GROUPS)
        y_prompt, c, h, kv = decoder_layer(y_prompt, p_prompt[i], jnp.zeros((bp, CONV_WIDTH - 1, D_RNN), dt),
                                           jnp.zeros((bp, D_RNN), dt), empty_kv, *lw)
        conv_p.append(c)
        h_p.append(h)
        for g, (window, _) in enumerate(GROUPS):
            kv_p[g].append(kv[g][:, -min(window, kv[g].shape[1]):])
        y_sample, c, h, kv = decoder_layer(y_sample, p_sample[i], state_conv[i], state_h[i],
                                           (cache_kv_w128[i], cache_kv_w512[i], cache_kv_w2048[i]), *lw)
        conv_s.append(c)
        h_s.append(h)
        for g in range(len(GROUPS)):
            kv_s[g].append(kv[g])
    return (y_prompt, y_sample, jnp.stack(conv_p), jnp.stack(conv_s), jnp.stack(h_p), jnp.stack(h_s),
            jnp.stack(kv_p[0]), jnp.stack(kv_s[0]), jnp.stack(kv_p[1]), jnp.stack(kv_s[1]),
            jnp.stack(kv_p[2]), jnp.stack(kv_s[2]))
```

```python
import functools
import math

import numpy as np
import jax
import jax.numpy as jnp
from jax import lax
from jax.experimental import pallas as pl
from jax.experimental.pallas import tpu as pltpu

F32 = jnp.float32
BF16 = jnp.bfloat16

D_MODEL = 1024
D_RNN = 1024
N_RNN_BLOCKS = 8
RNN_BLOCK = D_RNN // N_RNN_BLOCKS
CONV_WIDTH = 4
LRU_C = 8.0
GROUPS = ((128, 1), (512, 4), (2048, 16))
HEADS_PER_GROUP = 4
HEAD_DIM = 128
N_ATTN_HEADS = HEADS_PER_GROUP * len(GROUPS)
D_QKV = N_ATTN_HEADS * HEAD_DIM
D_ATTN_OUT = HEADS_PER_GROUP * HEAD_DIM
N_BUCKETS = 32
MAX_DISTANCE = 2048
QUERY_BLOCK = 128
D_IN = 2 * D_RNN + 3 * D_QKV + 2 * D_MODEL
N_EXPERTS = 32
TOP_K = 4
D_EXPERT = 1024
SWIGLU_ALPHA = 1.702
SWIGLU_LIMIT = 7.0
D_PLE = 256
EPS = 1e-6
ATTN_SCALE = HEAD_DIM ** -0.5
MASK_VALUE = -1e30

LANES = 128
COL_XR, COL_YG, COL_GA, COL_GB = 0, D_RNN, 2 * D_RNN, 2 * D_RNN + D_MODEL
COL_Q = 2 * D_RNN + 2 * D_MODEL
COL_K = COL_Q + D_QKV
COL_V = COL_K + D_QKV
INPROJ_TN = 512
ATTN_SUPER = 2048
EXPERT_ROWS = 256
VMEM_LIMIT = 56 * 1024 * 1024


def _cparams(n_axes):
    return pltpu.CompilerParams(dimension_semantics=("arbitrary",) * n_axes, vmem_limit_bytes=VMEM_LIMIT)


def _rms(x, g):
    ms = jnp.mean(x * x, axis=-1, keepdims=True)
    return (x * lax.rsqrt(ms + EPS)) * g


def _round_bf16(x):
    return x.astype(BF16).astype(F32)


def _inproj_kernel(x_ref, g_ref, w_ref, cg_ref, z_ref, u_scr):
    j = pl.program_id(1)

    @pl.when(j == 0)
    def _():
        u_scr[...] = _rms(x_ref[...], g_ref[...]).astype(BF16)

    acc = jnp.dot(u_scr[...], w_ref[...], preferred_element_type=F32)
    is_qk = jnp.logical_and(j >= COL_Q // INPROJ_TN, j < COL_V // INPROJ_TN)

    @pl.when(is_qk)
    def _():
        for h in range(INPROJ_TN // HEAD_DIM):
            cs = slice(h * HEAD_DIM, (h + 1) * HEAD_DIM)
            z_ref[:, cs] = _rms(acc[:, cs], cg_ref[:, cs]).astype(z_ref.dtype)

    @pl.when(jnp.logical_not(is_qk))
    def _():
        z_ref[...] = acc.astype(z_ref.dtype)


def _inproj(x2d, g, w_perm, colgain, tm, out_dtype):
    n = x2d.shape[0]
    return pl.pallas_call(
        _inproj_kernel,
        grid=(n // tm, D_IN // INPROJ_TN),
        in_specs=[
            pl.BlockSpec((tm, D_MODEL), lambda i, j: (i, 0)),
            pl.BlockSpec((1, D_MODEL), lambda i, j: (0, 0)),
            pl.BlockSpec((D_MODEL, INPROJ_TN), lambda i, j: (0, j)),
            pl.BlockSpec((1, INPROJ_TN), lambda i, j: (0, j)),
        ],
        out_specs=pl.BlockSpec((tm, INPROJ_TN), lambda i, j: (i, j)),
        out_shape=jax.ShapeDtypeStruct((n, D_IN), out_dtype),
        scratch_shapes=[pltpu.VMEM((tm, D_MODEL), BF16)],
        compiler_params=_cparams(2),
        name="inproj",
    )(x2d, g, w_perm, colgain)


def _softplus(v):
    return jnp.maximum(v, 0.0) + jnp.log1p(jnp.exp(-jnp.abs(v)))


def _lru_gates_block(xc, n, wax_ref, ba_ref, bx_ref, lam_ref):
    cs = slice(n * RNN_BLOCK, (n + 1) * RNN_BLOCK)
    ri = jnp.dot(xc.astype(BF16), wax_ref[n], preferred_element_type=F32)
    r = jax.nn.sigmoid(ri[:, :RNN_BLOCK] + ba_ref[:, cs])
    i = jax.nn.sigmoid(ri[:, RNN_BLOCK:] + bx_ref[:, cs])
    log_a = (-LRU_C * r) * _softplus(-lam_ref[:, cs])
    a = jnp.exp(log_a)
    b = jnp.sqrt(1.0 - jnp.exp(2.0 * log_a)) * (i * xc)
    return a, b


def _rglru_kernel(xr_ref, yg_ref, cw_ref, cb_ref, wax_ref, ba_ref, bx_ref, lam_ref,
                  aout_ref, hlast_ref, xe_scr, a_scr, b_scr, ac_scr, hcar_scr, *, tc):
    c = pl.program_id(1)
    seg = tc // 8

    @pl.when(c == 0)
    def _():
        xe_scr[0:8, :] = jnp.zeros((8, D_RNN), F32)
        hcar_scr[...] = jnp.zeros((1, D_RNN), F32)

    xe_scr[8:8 + tc, :] = xr_ref[...].astype(F32)
    for n in range(N_RNN_BLOCKS):
        cs = slice(n * RNN_BLOCK, (n + 1) * RNN_BLOCK)
        xc = cb_ref[:, cs]
        for j in range(CONV_WIDTH):
            xc = xc + xe_scr[pl.ds(8 - (CONV_WIDTH - 1) + j, tc), cs] * cw_ref[j:j + 1, cs]
        a, b = _lru_gates_block(xc, n, wax_ref, ba_ref, bx_ref, lam_ref)
        a_scr[n] = a
        b_scr[n] = b
    xe_scr[0:8, :] = xe_scr[tc:tc + 8, :]

    def step(j, carry):
        hs, accs = carry
        rows = pl.ds(j, 8, stride=seg)
        new_h, new_acc = [], []
        for n in range(N_RNN_BLOCKS):
            aj = a_scr[n, rows, :]
            h = aj * hs[n] + b_scr[n, rows, :]
            acc = accs[n] * aj
            b_scr[n, rows, :] = h
            ac_scr[n, rows, :] = acc
            new_h.append(h)
            new_acc.append(acc)
        return tuple(new_h), tuple(new_acc)

    init = (tuple(jnp.zeros((8, RNN_BLOCK), F32) for _ in range(N_RNN_BLOCKS)),
            tuple(jnp.ones((8, RNN_BLOCK), F32) for _ in range(N_RNN_BLOCKS)))
    h_end, a_end = lax.fori_loop(0, seg, step, init)

    for n in range(N_RNN_BLOCKS):
        cs = slice(n * RNN_BLOCK, (n + 1) * RNN_BLOCK)
        s = hcar_scr[:, cs]
        for k in range(8):
            rows = slice(k * seg, (k + 1) * seg)
            h = b_scr[n, rows, :] + ac_scr[n, rows, :] * s
            aout_ref[rows, cs] = (h * jax.nn.gelu(yg_ref[rows, cs].astype(F32))).astype(BF16)
            s = a_end[n][k:k + 1, :] * s + h_end[n][k:k + 1, :]
        hcar_scr[:, cs] = s
        hlast_ref[:, cs] = s


def _rglru_prompt(z, batch, seq, tc, cw, cb, wax, ba, bx, lam):
    nc = seq // tc
    vec = lambda r: pl.BlockSpec((r, D_RNN), lambda b, c: (0, 0))
    return pl.pallas_call(
        functools.partial(_rglru_kernel, tc=tc),
        grid=(batch, nc),
        in_specs=[
            pl.BlockSpec((tc, D_RNN), lambda b, c: (b * nc + c, COL_XR // D_RNN)),
            pl.BlockSpec((tc, D_RNN), lambda b, c: (b * nc + c, COL_YG // D_RNN)),
            vec(CONV_WIDTH), vec(1),
            pl.BlockSpec((N_RNN_BLOCKS, RNN_BLOCK, 2 * RNN_BLOCK), lambda b, c: (0, 0, 0)),
            vec(1), vec(1), vec(1),
        ],
        out_specs=[
            pl.BlockSpec((tc, D_RNN), lambda b, c: (b * nc + c, 0)),
            pl.BlockSpec((None, 1, D_RNN), lambda b, c: (b, 0, 0)),
        ],
        out_shape=[jax.ShapeDtypeStruct((batch * seq, D_RNN), BF16),
                   jax.ShapeDtypeStruct((batch, 1, D_RNN), F32)],
        scratch_shapes=[pltpu.VMEM((tc + 8, D_RNN), F32)]
        + [pltpu.VMEM((N_RNN_BLOCKS, tc, RNN_BLOCK), F32)] * 3
        + [pltpu.VMEM((1, D_RNN), F32)],
        compiler_params=_cparams(2),
        name="rglru",
    )(z, z, cw, cb, wax, ba, bx, lam)


def _rglru_step_kernel(xr_ref, yg_ref, c0_ref, c1_ref, c2_ref, h0_ref, cw_ref, cb_ref, wax_ref, ba_ref,
                       bx_ref, lam_ref, aout_ref, hout_ref):
    taps = (c0_ref, c1_ref, c2_ref)
    for n in range(N_RNN_BLOCKS):
        cs = slice(n * RNN_BLOCK, (n + 1) * RNN_BLOCK)
        xc = cb_ref[:, cs]
        for j in range(CONV_WIDTH - 1):
            xc = xc + taps[j][:, cs] * cw_ref[j:j + 1, cs]
        xc = xc + xr_ref[:, cs].astype(F32) * cw_ref[CONV_WIDTH - 1:CONV_WIDTH, cs]
        a, b = _lru_gates_block(xc, n, wax_ref, ba_ref, bx_ref, lam_ref)
        h = a * h0_ref[:, cs] + b
        hout_ref[:, cs] = h
        aout_ref[:, cs] = (h * jax.nn.gelu(yg_ref[:, cs].astype(F32))).astype(BF16)


def _rglru_step(z, conv_taps, h0, cw, cb, wax, ba, bx, lam):
    n = z.shape[0]
    full = lambda shape: pl.BlockSpec(shape, lambda i: (0,) * len(shape))
    return pl.pallas_call(
        _rglru_step_kernel,
        grid=(1,),
        in_specs=[
            pl.BlockSpec((n, D_RNN), lambda i: (0, COL_XR // D_RNN)),
            pl.BlockSpec((n, D_RNN), lambda i: (0, COL_YG // D_RNN)),
            full((n, D_RNN)), full((n, D_RNN)), full((n, D_RNN)), full((n, D_RNN)),
            full((CONV_WIDTH, D_RNN)), full((1, D_RNN)),
            full((N_RNN_BLOCKS, RNN_BLOCK, 2 * RNN_BLOCK)),
            full((1, D_RNN)), full((1, D_RNN)), full((1, D_RNN)),
        ],
        out_specs=[full((n, D_RNN)), full((n, D_RNN))],
        out_shape=[jax.ShapeDtypeStruct((n, D_RNN), BF16), jax.ShapeDtypeStruct((n, D_RNN), F32)],
        compiler_params=_cparams(1),
        name="rglru_step",
    )(z, z, *conv_taps, h0, cw, cb, wax, ba, bx, lam)


def _t5_causal_buckets(dist):
    max_exact = N_BUCKETS // 2
    d = np.maximum(dist, 1).astype(np.float32)
    large = max_exact + (np.log(d / max_exact) / np.log(MAX_DISTANCE / max_exact)
                         * (N_BUCKETS - max_exact)).astype(np.int32)
    large = np.minimum(large, N_BUCKETS - 1)
    return np.where(dist < max_exact, dist, large).astype(np.int32)


def _group_bias(rel_bias, gi, window, dilation):
    buckets = _t5_causal_buckets(np.arange(window // dilation + 1) * dilation)
    heads = slice(gi * HEADS_PER_GROUP, (gi + 1) * HEADS_PER_GROUP)
    return rel_bias[buckets][:, heads].T.astype(F32)


def _band_bias(bias):
    nk = bias.shape[1]
    dist = np.arange(QUERY_BLOCK)[:, None] + (nk - 1) - np.arange(2 * QUERY_BLOCK)[None, :]
    valid = (dist >= 0) & (dist < nk)
    return jnp.where(valid[None], bias[:, np.clip(dist, 0, nk - 1)], MASK_VALUE)


def _attn_kernel(q_ref, k_ref, v_ref, bias_ref, o_ref, lse_ref, q_scr, kc_scr, vc_scr, o_scr, l_scr, *, d, sup):
    sb = pl.program_id(2)
    win = QUERY_BLOCK * d

    @pl.when(sb == 0)
    def _():
        kc_scr[0:win, :] = jnp.zeros((win, HEAD_DIM), F32)
        vc_scr[0:win, :] = jnp.zeros((win, HEAD_DIM), F32)

    q_scr[...] = q_ref[...].astype(F32)
    kc_scr[win:win + sup, :] = k_ref[...].astype(F32)
    vc_scr[win:win + sup, :] = v_ref[...].astype(F32)
    bias = bias_ref[...]
    col = lax.broadcasted_iota(jnp.int32, (QUERY_BLOCK, 2 * QUERY_BLOCK), 1)

    def rows(start, size):
        return pl.ds(start, size) if d == 1 else pl.ds(start, size, stride=d)

    for r in range(d):
        for qb in range(sup // win):
            start = r + win * qb
            qu = q_scr[rows(start, QUERY_BLOCK), :].astype(BF16)
            ku = kc_scr[rows(start, 2 * QUERY_BLOCK), :].astype(BF16)
            vu = vc_scr[rows(start, 2 * QUERY_BLOCK), :].astype(BF16)
            s = lax.dot_general(qu, ku, (((1,), (1,)), ((), ())), preferred_element_type=F32) * ATTN_SCALE + bias
            if qb == 0:
                s = jnp.where(jnp.logical_or(sb > 0, col >= QUERY_BLOCK), s, MASK_VALUE)
            m = jnp.max(s, axis=-1, keepdims=True)
            e = jnp.exp(s - m)
            den = jnp.sum(e, axis=-1, keepdims=True)
            o_scr[rows(start, QUERY_BLOCK), :] = jnp.dot((e / den).astype(BF16), vu, preferred_element_type=F32)
            l_scr[rows(start, QUERY_BLOCK), :] = jnp.broadcast_to(m + jnp.log(den), (QUERY_BLOCK, HEAD_DIM))

    o_ref[...] = o_scr[...].astype(BF16)
    lse_ref[...] = l_scr[...]
    kc_scr[0:win, :] = kc_scr[sup:sup + win, :]
    vc_scr[0:win, :] = vc_scr[sup:sup + win, :]


def _attn_prompt(z, band, gi, batch, seq):
    d = GROUPS[gi][1]
    sup = ATTN_SUPER
    win = QUERY_BLOCK * d
    nsb = seq // sup
    qb0 = COL_Q // HEAD_DIM + gi * HEADS_PER_GROUP
    kb0 = COL_K // HEAD_DIM + gi * HEADS_PER_GROUP
    vb0 = COL_V // HEAD_DIM + gi * HEADS_PER_GROUP
    blk = lambda c0: pl.BlockSpec((sup, HEAD_DIM), lambda b, h, s: (b * nsb + s, c0 + h))
    out = pl.BlockSpec((sup, HEAD_DIM), lambda b, h, s: (b * nsb + s, h))
    return pl.pallas_call(
        functools.partial(_attn_kernel, d=d, sup=sup),
        grid=(batch, HEADS_PER_GROUP, nsb),
        in_specs=[blk(qb0), blk(kb0), blk(vb0),
                  pl.BlockSpec((None, QUERY_BLOCK, 2 * QUERY_BLOCK), lambda b, h, s: (h, 0, 0))],
        out_specs=[out, out],
        out_shape=[jax.ShapeDtypeStruct((batch * seq, D_ATTN_OUT), BF16),
                   jax.ShapeDtypeStruct((batch * seq, D_ATTN_OUT), F32)],
        scratch_shapes=[pltpu.VMEM((sup, HEAD_DIM), F32), pltpu.VMEM((win + sup, HEAD_DIM), F32),
                        pltpu.VMEM((win + sup, HEAD_DIM), F32), pltpu.VMEM((sup, HEAD_DIM), F32),
                        pltpu.VMEM((sup, HEAD_DIM), F32)],
        compiler_params=_cparams(3),
        name=f"attn_g{gi}",
    )(z, z, z, band)


def _attn_step_kernel(q_ref, kn_ref, vn_ref, c0_ref, c1_ref, c2_ref, bc_ref, bn_ref, o_ref, lse_ref, *, bb):
    caches = (c0_ref, c1_ref, c2_ref)
    for b in range(bb):
        for g in range(len(GROUPS)):
            hs = slice(g * HEADS_PER_GROUP, (g + 1) * HEADS_PER_GROUP)
            qg = _round_bf16(q_ref[b, hs, :])
            kc = _round_bf16(caches[g][b, :, 0])
            vc = _round_bf16(caches[g][b, :, 1])
            kn = _round_bf16(kn_ref[b, hs, :])
            vn = _round_bf16(vn_ref[b, hs, :])
            sc = jnp.sum(kc * qg[None], axis=-1, keepdims=True) * ATTN_SCALE + bc_ref[g]
            sn = jnp.sum(kn * qg, axis=-1, keepdims=True) * ATTN_SCALE + bn_ref[g]
            m = jnp.maximum(jnp.max(sc, axis=0), sn)
            ec = jnp.exp(sc - m[None])
            en = jnp.exp(sn - m)
            den = jnp.sum(ec, axis=0) + en
            o = jnp.sum(_round_bf16(ec / den[None]) * vc, axis=0) + _round_bf16(en / den) * vn
            o_ref[g, b] = o
            lse_ref[g, b] = jnp.broadcast_to(m + jnp.log(den), (HEADS_PER_GROUP, HEAD_DIM))


def _attn_step(q, kn, vn, caches, bias_c, bias_n, bb):
    n = q.shape[0]
    ng = len(GROUPS)
    tok = pl.BlockSpec((bb, N_ATTN_HEADS, HEAD_DIM), lambda i: (i, 0, 0))
    cache = pl.BlockSpec((bb, QUERY_BLOCK, None, 2, HEADS_PER_GROUP, HEAD_DIM), lambda i: (i, 0, 0, 0, 0, 0))
    out = pl.BlockSpec((ng, bb, HEADS_PER_GROUP, HEAD_DIM), lambda i: (0, i, 0, 0))
    return pl.pallas_call(
        functools.partial(_attn_step_kernel, bb=bb),
        grid=(n // bb,),
        in_specs=[tok, tok, tok, cache, cache, cache,
                  pl.BlockSpec((ng, QUERY_BLOCK, HEADS_PER_GROUP, 1), lambda i: (0, 0, 0, 0)),
                  pl.BlockSpec((ng, HEADS_PER_GROUP, 1), lambda i: (0, 0, 0))],
        out_specs=[out, out],
        out_shape=[jax.ShapeDtypeStruct((ng, n, HEADS_PER_GROUP, HEAD_DIM), F32)] * 2,
        compiler_params=_cparams(1),
        name="attn_step",
    )(q, kn, vn, *caches, bias_c, bias_n)


def _mix_kernel(a_ref, o0_ref, o1_ref, o2_ref, l0_ref, l1_ref, l2_ref, ga_ref, gb_ref, x_ref,
                wr_ref, wa_ref, wo_ref, gf_ref, wrt_ref, brt_ref,
                x1_ref, xn_ref, gate_ref, idx_ref):
    tm = x_ref.shape[0]
    l0, l1, l2 = l0_ref[...], l1_ref[...], l2_ref[...]
    mx = jnp.maximum(jnp.maximum(l0, l1), l2)
    e0, e1, e2 = jnp.exp(l0 - mx), jnp.exp(l1 - mx), jnp.exp(l2 - mx)
    den = e0 + e1 + e2
    b_out = (_round_bf16(e0 / den) * _round_bf16(o0_ref[...]) + _round_bf16(e1 / den) * _round_bf16(o1_ref[...])
             + _round_bf16(e2 / den) * _round_bf16(o2_ref[...]))
    rnn = jnp.dot(a_ref[...], wr_ref[...], preferred_element_type=F32)
    att = jnp.dot(b_out.astype(BF16), wa_ref[...], preferred_element_type=F32)
    merged = jax.nn.sigmoid(ga_ref[...].astype(F32)) * rnn + jax.nn.sigmoid(gb_ref[...].astype(F32)) * att
    x1 = x_ref[...] + jnp.dot(merged.astype(BF16), wo_ref[...], preferred_element_type=F32)
    x1_ref[...] = x1
    xn = _rms(x1, gf_ref[...]).astype(BF16)
    xn_ref[...] = xn

    logits = jnp.dot(xn, wrt_ref[...], preferred_element_type=F32) + brt_ref[...]
    lane = lax.broadcasted_iota(jnp.int32, (tm, LANES), 1)
    cur = jnp.where(lane < N_EXPERTS, logits, -jnp.inf)
    vals, idxs = [], []
    for _ in range(TOP_K):
        mk = jnp.max(cur, axis=-1, keepdims=True)
        ik = jnp.min(jnp.where(cur == mk, lane, LANES), axis=-1, keepdims=True)
        vals.append(mk)
        idxs.append(ik)
        cur = jnp.where(lane == ik, -jnp.inf, cur)
    es = [jnp.exp(v - vals[0]) for v in vals]
    den = es[0] + es[1] + es[2] + es[3]
    gate_out = jnp.zeros((tm, LANES), F32)
    idx_out = jnp.zeros((tm, LANES), jnp.int32)
    for k in range(TOP_K):
        gate_out = jnp.where(lane == k, es[k] / den, gate_out)
        idx_out = jnp.where(lane == k, idxs[k], idx_out)
    gate_ref[...] = gate_out
    idx_ref[...] = idx_out


def _mix(a_out, os_, ls_, z, x2d, wr, wa, wo, gf, wrt, brt, tm):
    n = x2d.shape[0]
    row = lambda w, c=0: pl.BlockSpec((tm, w), lambda i: (i, c))
    full = lambda shape: pl.BlockSpec(shape, lambda i: (0,) * len(shape))
    return pl.pallas_call(
        _mix_kernel,
        grid=(n // tm,),
        in_specs=[row(D_RNN), row(D_ATTN_OUT), row(D_ATTN_OUT), row(D_ATTN_OUT),
                  row(D_ATTN_OUT), row(D_ATTN_OUT), row(D_ATTN_OUT),
                  row(D_MODEL, COL_GA // D_MODEL), row(D_MODEL, COL_GB // D_MODEL), row(D_MODEL),
                  full((D_RNN, D_MODEL)), full((D_ATTN_OUT, D_MODEL)), full((D_MODEL, D_MODEL)),
                  full((1, D_MODEL)), full((D_MODEL, LANES)), full((1, LANES))],
        out_specs=[row(D_MODEL), row(D_MODEL), row(LANES), row(LANES)],
        out_shape=[jax.ShapeDtypeStruct((n, D_MODEL), F32), jax.ShapeDtypeStruct((n, D_MODEL), BF16),
                   jax.ShapeDtypeStruct((n, LANES), F32), jax.ShapeDtypeStruct((n, LANES), jnp.int32)],
        compiler_params=_cparams(1),
        name="mix",
    )(a_out, *os_, *ls_, z, z, x2d, wr, wa, wo, gf, wrt, brt)


def _moe_kernel(be_ref, nu_ref, xb_ref, gate_ref, wgu_ref, bgu_ref, wd_ref, bd_ref, y_ref, wgu_bf, wd_bf):
    i = pl.program_id(0)
    changed = jnp.logical_or(i == 0, be_ref[i] != be_ref[jnp.maximum(i - 1, 0)])
    active = i < nu_ref[0]

    @pl.when(jnp.logical_and(active, changed))
    def _():
        wgu_bf[...] = wgu_ref[...].astype(BF16)
        wd_bf[...] = wd_ref[...].astype(BF16)

    @pl.when(active)
    def _():
        gu = jnp.dot(xb_ref[...], wgu_bf[...], preferred_element_type=F32) + bgu_ref[...]
        g = jnp.minimum(gu[:, :D_EXPERT], SWIGLU_LIMIT)
        u = jnp.clip(gu[:, D_EXPERT:], -SWIGLU_LIMIT, SWIGLU_LIMIT)
        act = (u + 1.0) * (g * jax.nn.sigmoid(SWIGLU_ALPHA * g))
        y = jnp.dot(act.astype(BF16), wd_bf[...], preferred_element_type=F32) + bd_ref[...]
        y_ref[...] = (y * gate_ref[...]).astype(BF16)

    @pl.when(jnp.logical_not(active))
    def _():
        y_ref[...] = jnp.zeros(y_ref.shape, BF16)


def _moe(block_e, n_used, xb, row_gate, wgu, bgu, wd, bd):
    rows = xb.shape[0]
    eb = EXPERT_ROWS
    grid_spec = pltpu.PrefetchScalarGridSpec(
        num_scalar_prefetch=2,
        grid=(rows // eb,),
        in_specs=[
            pl.BlockSpec((eb, D_MODEL), lambda i, be, nu: (i, 0)),
            pl.BlockSpec((eb, 1), lambda i, be, nu: (i, 0)),
            pl.BlockSpec((None, D_MODEL, 2 * D_EXPERT), lambda i, be, nu: (be[i], 0, 0)),
            pl.BlockSpec((None, 1, 2 * D_EXPERT), lambda i, be, nu: (be[i], 0, 0)),
            pl.BlockSpec((None, D_EXPERT, D_MODEL), lambda i, be, nu: (be[i], 0, 0)),
            pl.BlockSpec((None, 1, D_MODEL), lambda i, be, nu: (be[i], 0, 0)),
        ],
        out_specs=pl.BlockSpec((eb, D_MODEL), lambda i, be, nu: (i, 0)),
        scratch_shapes=[pltpu.VMEM((D_MODEL, 2 * D_EXPERT), BF16), pltpu.VMEM((D_EXPERT, D_MODEL), BF16)],
    )
    return pl.pallas_call(
        _moe_kernel,
        grid_spec=grid_spec,
        out_shape=jax.ShapeDtypeStruct((rows, D_MODEL), BF16),
        compiler_params=_cparams(1),
        name="moe",
    )(block_e, n_used, xb, row_gate, wgu, bgu, wd, bd)


def _route(idx, gates, eb):
    n_assign = idx.shape[0] * TOP_K
    flat_e = idx.reshape(-1)
    order = jnp.argsort(flat_e, stable=True).astype(jnp.int32)
    e_sorted = flat_e[order]
    counts = jnp.bincount(flat_e, length=N_EXPERTS).astype(jnp.int32)
    padded = (counts + eb - 1) // eb * eb
    pad_end = jnp.cumsum(padded)
    pad_start = pad_end - padded
    start = jnp.cumsum(counts) - counts
    dest = pad_start[e_sorted] + jnp.arange(n_assign, dtype=jnp.int32) - start[e_sorted]
    n_blocks = -(-n_assign // eb) + N_EXPERTS
    rows = n_blocks * eb
    row_src = jnp.zeros((rows,), jnp.int32).at[dest].set(order)
    row_gate = jnp.zeros((rows,), F32).at[dest].set(gates.reshape(-1)[order])
    block_e = jnp.minimum(jnp.searchsorted(pad_end, jnp.arange(n_blocks, dtype=jnp.int32) * eb, side='right'),
                          N_EXPERTS - 1).astype(jnp.int32)
    n_used = (pad_end[-1:] // eb).astype(jnp.int32)
    row_of_assign = jnp.zeros((n_assign,), jnp.int32).at[order].set(dest)
    return row_src, row_gate, block_e, n_used, row_of_assign


def _ple_kernel(x1_ref, yc_ref, p_ref, gp_ref, wg_ref, wp_ref, y_ref):
    moe = yc_ref[:, 0:D_MODEL].astype(F32)
    for k in range(1, TOP_K):
        moe = moe + yc_ref[:, k * D_MODEL:(k + 1) * D_MODEL].astype(F32)
    x2 = x1_ref[...] + moe
    gate = jax.nn.sigmoid(jnp.dot(_rms(x2, gp_ref[...]).astype(BF16), wg_ref[...], preferred_element_type=F32))
    proj = jnp.dot(p_ref[...].astype(BF16), wp_ref[...], preferred_element_type=F32)
    y_ref[...] = x2 + gate * proj


def _ple(x1, yc, p2d, gp, wg, wp, tm):
    n = x1.shape[0]
    row = lambda w: pl.BlockSpec((tm, w), lambda i: (i, 0))
    full = lambda shape: pl.BlockSpec(shape, lambda i: (0,) * len(shape))
    return pl.pallas_call(
        _ple_kernel,
        grid=(n // tm,),
        in_specs=[row(D_MODEL), row(TOP_K * D_MODEL), row(D_PLE), full((1, D_MODEL)),
                  full((D_MODEL, D_MODEL)), full((D_PLE, D_MODEL))],
        out_specs=row(D_MODEL),
        out_shape=jax.ShapeDtypeStruct((n, D_MODEL), F32),
        compiler_params=_cparams(1),
        name="ple",
    )(x1, yc, p2d, gp, wg, wp)


def kernel(x_prompt, x_sample, state_conv, state_h, cache_kv_w128, cache_kv_w512, cache_kv_w2048, p_prompt,
           p_sample, norm_mix_g, w_in, conv_w, conv_b, w_rg_a, b_rg_a, w_rg_x, b_rg_x, lru_lambda, q_norm_g,
           k_norm_g, rel_bias, w_br_rnn, w_br_attn, w_out, norm_ffn_g, w_router, b_router, w_gate_up, b_gate_up,
           w_down, b_down, norm_ple_g, w_ple_gate, w_ple_proj):
    bp, seq, _ = x_prompt.shape
    bs = x_sample.shape[0]
    n_p = bp * seq
    li = 0
    row = lambda v: v.reshape(1, -1).astype(F32)

    w = w_in[li]
    o_q, o_ga = 2 * D_RNN, 2 * D_RNN + 3 * D_QKV
    w_perm = jnp.concatenate([w[:, :o_q], w[:, o_ga:], w[:, o_q:o_ga]], axis=1).astype(BF16)
    colgain = jnp.concatenate([jnp.ones((COL_Q,), F32), jnp.tile(q_norm_g[li], N_ATTN_HEADS),
                               jnp.tile(k_norm_g[li], N_ATTN_HEADS), jnp.ones((D_QKV,), F32)]).reshape(1, D_IN)
    wax = jnp.concatenate([w_rg_a[li], w_rg_x[li]], axis=-1).astype(BF16)
    lru = (conv_w[li].astype(F32), row(conv_b[li]), wax, row(b_rg_a[li]), row(b_rg_x[li]), row(lru_lambda[li]))
    wr, wa, wo = w_br_rnn[li].astype(BF16), w_br_attn[li].astype(BF16), w_out[li].astype(BF16)
    wrt = jnp.pad(w_router[li].astype(BF16), ((0, 0), (0, LANES - N_EXPERTS)))
    brt = jnp.pad(row(b_router[li]), ((0, 0), (0, LANES - N_EXPERTS)))
    gf = row(norm_ffn_g[li])
    biases = [_group_bias(rel_bias, gi, wnd, dil) for gi, (wnd, dil) in enumerate(GROUPS)]

    xp = x_prompt.reshape(n_p, D_MODEL)
    zp = _inproj(xp, row(norm_mix_g[li]), w_perm, colgain, tm=512, out_dtype=BF16)
    a_p, h_p = _rglru_prompt(zp, bp, seq, 512, *lru)
    os_p, ls_p = [], []
    for gi in range(len(GROUPS)):
        o, l = _attn_prompt(zp, _band_bias(biases[gi]), gi, bp, seq)
        os_p.append(o)
        ls_p.append(l)
    x1_p, xn_p, gate_p, idx_p = _mix(a_p, os_p, ls_p, zp, xp, wr, wa, wo, gf, wrt, brt, tm=256)

    xs = x_sample.reshape(bs, D_MODEL)
    zs = _inproj(xs, row(norm_mix_g[li]), w_perm, colgain, tm=bs, out_dtype=F32)
    sc = state_conv[li]
    a_s, h_s = _rglru_step(zs, [sc[:, j] for j in range(CONV_WIDTH - 1)], state_h[li], *lru)
    heads = lambda c0: zs[:, c0:c0 + D_QKV].astype(F32).reshape(bs, N_ATTN_HEADS, HEAD_DIM)
    caches = [c[li].reshape(bs, QUERY_BLOCK, dil, 2, HEADS_PER_GROUP, HEAD_DIM)
              for c, (_, dil) in zip((cache_kv_w128, cache_kv_w512, cache_kv_w2048), GROUPS)]
    bias_c = jnp.stack([b[:, :0:-1].T for b in biases])[..., None]
    bias_n = jnp.stack([b[:, 0] for b in biases])[..., None]
    o_s, l_s = _attn_step(heads(COL_Q), heads(COL_K), heads(COL_V), caches, bias_c, bias_n, bb=4)
    os_s = [o_s[g].reshape(bs, D_ATTN_OUT) for g in range(len(GROUPS))]
    ls_s = [l_s[g].reshape(bs, D_ATTN_OUT) for g in range(len(GROUPS))]
    x1_s, xn_s, gate_s, idx_s = _mix(a_s, os_s, ls_s, zs, xs, wr, wa, wo, gf, wrt, brt, tm=bs)

    idx = jnp.concatenate([idx_p[:, :TOP_K], idx_s[:, :TOP_K]], axis=0)
    gates = jnp.concatenate([gate_p[:, :TOP_K], gate_s[:, :TOP_K]], axis=0)
    row_src, row_gate, block_e, n_used, row_of_assign = _route(idx, gates, EXPERT_ROWS)
    xn_all = jnp.concatenate([xn_p, xn_s], axis=0)
    xb = xn_all[row_src // TOP_K]
    yb = _moe(block_e, n_used, xb, row_gate.reshape(-1, 1), w_gate_up[li], b_gate_up[li][:, None, :],
              w_down[li], b_down[li][:, None, :])
    yc = yb[row_of_assign].reshape(n_p + bs, TOP_K * D_MODEL)

    gp, wg, wp = row(norm_ple_g[li]), w_ple_gate[li].astype(BF16), w_ple_proj[li].astype(BF16)
    y_p = _ple(x1_p, yc[:n_p], p_prompt[li].reshape(n_p, D_PLE), gp, wg, wp, tm=256)
    y_s = _ple(x1_s, yc[n_p:], p_sample[li].reshape(bs, D_PLE), gp, wg, wp, tm=bs)

    z3 = zp.reshape(bp, seq, D_IN)
    zs3 = zs.reshape(bs, 1, D_IN)

    def new_kv(zz, gi, keep):
        t = zz.shape[1]
        c = gi * D_ATTN_OUT
        kk = zz[:, t - keep:, COL_K + c:COL_K + c + D_ATTN_OUT].astype(F32)
        vv = zz[:, t - keep:, COL_V + c:COL_V + c + D_ATTN_OUT].astype(F32)
        shape = (zz.shape[0], keep, HEADS_PER_GROUP, HEAD_DIM)
        return jnp.stack([kk.reshape(shape), vv.reshape(shape)], axis=2)[None]

    conv_p = z3[:, seq - (CONV_WIDTH - 1):, COL_XR:COL_XR + D_RNN].astype(F32)[None]
    conv_s = jnp.concatenate([sc[:, 1:], zs3[:, :, COL_XR:COL_XR + D_RNN].astype(F32)], axis=1)[None]
    outs = [y_p.reshape(bp, seq, D_MODEL), y_s.reshape(bs, 1, D_MODEL), conv_p, conv_s,
            h_p.reshape(1, bp, D_RNN), h_s.reshape(1, bs, D_RNN)]
    for gi, (wnd, _) in enumerate(GROUPS):
        outs.append(new_kv(z3, gi, min(wnd, seq)))
        outs.append(new_kv(zs3, gi, 1))
    return tuple(outs)
```

```python
import functools
import math

import numpy as np
import jax
import jax.numpy as jnp
from jax import lax
from jax.experimental import pallas as pl
from jax.experimental.pallas import tpu as pltpu

F32 = jnp.float32
BF16 = jnp.bfloat16

D_MODEL = 1024
D_RNN = 1024
N_RNN_BLOCKS = 8
RNN_BLOCK = D_RNN // N_RNN_BLOCKS
CONV_WIDTH = 4
LRU_C = 8.0
GROUPS = ((128, 1), (512, 4), (2048, 16))
HEADS_PER_GROUP = 4
HEAD_DIM = 128
N_ATTN_HEADS = HEADS_PER_GROUP * len(GROUPS)
D_QKV = N_ATTN_HEADS * HEAD_DIM
D_ATTN_OUT = HEADS_PER_GROUP * HEAD_DIM
N_BUCKETS = 32
MAX_DISTANCE = 2048
QUERY_BLOCK = 128
D_IN = 2 * D_RNN + 3 * D_QKV + 2 * D_MODEL
N_EXPERTS = 32
TOP_K = 4
D_EXPERT = 1024
SWIGLU_ALPHA = 1.702
SWIGLU_LIMIT = 7.0
D_PLE = 256
EPS = 1e-6
ATTN_SCALE = HEAD_DIM ** -0.5
MASK_VALUE = -1e30

LANES = 128
COL_XR, COL_YG, COL_GA, COL_GB = 0, D_RNN, 2 * D_RNN, 2 * D_RNN + D_MODEL
COL_Q = 2 * D_RNN + 2 * D_MODEL
COL_K = COL_Q + D_QKV
COL_V = COL_K + D_QKV
INPROJ_TN = 512
ATTN_SUPER = 2048
EXPERT_ROWS = 256
VMEM_LIMIT = 56 * 1024 * 1024


def _cparams(n_axes):
    return pltpu.CompilerParams(dimension_semantics=("arbitrary",) * n_axes, vmem_limit_bytes=VMEM_LIMIT)


def _rms(x, g):
    ms = jnp.mean(x * x, axis=-1, keepdims=True)
    return (x * lax.rsqrt(ms + EPS)) * g


def _round_bf16(x):
    return x.astype(BF16).astype(F32)


def _inproj_kernel(x_ref, g_ref, w_ref, cg_ref, z_ref, u_scr):
    j = pl.program_id(1)

    @pl.when(j == 0)
    def _():
        u_scr[...] = _rms(x_ref[...], g_ref[...]).astype(BF16)

    acc = jnp.dot(u_scr[...], w_ref[...], preferred_element_type=F32)
    is_qk = jnp.logical_and(j >= COL_Q // INPROJ_TN, j < COL_V // INPROJ_TN)

    @pl.when(is_qk)
    def _():
        for h in range(INPROJ_TN // HEAD_DIM):
            cs = slice(h * HEAD_DIM, (h + 1) * HEAD_DIM)
            z_ref[:, cs] = _rms(acc[:, cs], cg_ref[:, cs]).astype(z_ref.dtype)

    @pl.when(jnp.logical_not(is_qk))
    def _():
        z_ref[...] = acc.astype(z_ref.dtype)


def _inproj(x2d, g, w_perm, colgain, tm, out_dtype):
    n = x2d.shape[0]
    return pl.pallas_call(
        _inproj_kernel,
        grid=(n // tm, D_IN // INPROJ_TN),
        in_specs=[
            pl.BlockSpec((tm, D_MODEL), lambda i, j: (i, 0)),
            pl.BlockSpec((1, D_MODEL), lambda i, j: (0, 0)),
            pl.BlockSpec((D_MODEL, INPROJ_TN), lambda i, j: (0, j)),
            pl.BlockSpec((1, INPROJ_TN), lambda i, j: (0, j)),
        ],
        out_specs=pl.BlockSpec((tm, INPROJ_TN), lambda i, j: (i, j)),
        out_shape=jax.ShapeDtypeStruct((n, D_IN), out_dtype),
        scratch_shapes=[pltpu.VMEM((tm, D_MODEL), BF16)],
        compiler_params=_cparams(2),
        name="inproj",
    )(x2d, g, w_perm, colgain)


def _softplus(v):
    return jnp.maximum(v, 0.0) + jnp.log1p(jnp.exp(-jnp.abs(v)))


def _lru_gates_block(xc, n, wax_ref, ba_ref, bx_ref, lam_ref):
    cs = slice(n * RNN_BLOCK, (n + 1) * RNN_BLOCK)
    ri = jnp.dot(xc.astype(BF16), wax_ref[n], preferred_element_type=F32)
    r = jax.nn.sigmoid(ri[:, :RNN_BLOCK] + ba_ref[:, cs])
    i = jax.nn.sigmoid(ri[:, RNN_BLOCK:] + bx_ref[:, cs])
    log_a = (-LRU_C * r) * _softplus(-lam_ref[:, cs])
    a = jnp.exp(log_a)
    b = jnp.sqrt(1.0 - jnp.exp(2.0 * log_a)) * (i * xc)
    return a, b


def _rglru_kernel(xr_ref, yg_ref, cw_ref, cb_ref, wax_ref, ba_ref, bx_ref, lam_ref,
                  aout_ref, hlast_ref, xe_scr, a_scr, b_scr, ac_scr, hcar_scr, *, tc):
    c = pl.program_id(1)
    seg = tc // 8

    @pl.when(c == 0)
    def _():
        xe_scr[0:8, :] = jnp.zeros((8, D_RNN), F32)
        hcar_scr[...] = jnp.zeros((1, D_RNN), F32)

    xe_scr[8:8 + tc, :] = xr_ref[...].astype(F32)
    for n in range(N_RNN_BLOCKS):
        cs = slice(n * RNN_BLOCK, (n + 1) * RNN_BLOCK)
        xc = cb_ref[:, cs]
        for j in range(CONV_WIDTH):
            xc = xc + xe_scr[pl.ds(8 - (CONV_WIDTH - 1) + j, tc), cs] * cw_ref[j:j + 1, cs]
        a, b = _lru_gates_block(xc, n, wax_ref, ba_ref, bx_ref, lam_ref)
        a_scr[n] = a
        b_scr[n] = b
    xe_scr[0:8, :] = xe_scr[tc:tc + 8, :]

    def step(j, carry):
        hs, accs = carry
        rows = pl.ds(j, 8, stride=seg)
        new_h, new_acc = [], []
        for n in range(N_RNN_BLOCKS):
            aj = a_scr[n, rows, :]
            h = aj * hs[n] + b_scr[n, rows, :]
            acc = accs[n] * aj
            b_scr[n, rows, :] = h
            ac_scr[n, rows, :] = acc
            new_h.append(h)
            new_acc.append(acc)
        return tuple(new_h), tuple(new_acc)

    init = (tuple(jnp.zeros((8, RNN_BLOCK), F32) for _ in range(N_RNN_BLOCKS)),
            tuple(jnp.ones((8, RNN_BLOCK), F32) for _ in range(N_RNN_BLOCKS)))
    h_end, a_end = lax.fori_loop(0, seg, step, init)

    for n in range(N_RNN_BLOCKS):
        cs = slice(n * RNN_BLOCK, (n + 1) * RNN_BLOCK)
        s = hcar_scr[:, cs]
        for k in range(8):
            rows = slice(k * seg, (k + 1) * seg)
            h = b_scr[n, rows, :] + ac_scr[n, rows, :] * s
            aout_ref[rows, cs] = (h * jax.nn.gelu(yg_ref[rows, cs].astype(F32))).astype(BF16)
            s = a_end[n][k:k + 1, :] * s + h_end[n][k:k + 1, :]
        hcar_scr[:, cs] = s
        hlast_ref[:, cs] = s


def _rglru_prompt(z, batch, seq, tc, cw, cb, wax, ba, bx, lam):
    nc = seq // tc
    vec = lambda r: pl.BlockSpec((r, D_RNN), lambda b, c: (0, 0))
    return pl.pallas_call(
        functools.partial(_rglru_kernel, tc=tc),
        grid=(batch, nc),
        in_specs=[
            pl.BlockSpec((tc, D_RNN), lambda b, c: (b * nc + c, COL_XR // D_RNN)),
            pl.BlockSpec((tc, D_RNN), lambda b, c: (b * nc + c, COL_YG // D_RNN)),
            vec(CONV_WIDTH), vec(1),
            pl.BlockSpec((N_RNN_BLOCKS, RNN_BLOCK, 2 * RNN_BLOCK), lambda b, c: (0, 0, 0)),
            vec(1), vec(1), vec(1),
        ],
        out_specs=[
            pl.BlockSpec((tc, D_RNN), lambda b, c: (b * nc + c, 0)),
            pl.BlockSpec((None, 1, D_RNN), lambda b, c: (b, 0, 0)),
        ],
        out_shape=[jax.ShapeDtypeStruct((batch * seq, D_RNN), BF16),
                   jax.ShapeDtypeStruct((batch, 1, D_RNN), F32)],
        scratch_shapes=[pltpu.VMEM((tc + 8, D_RNN), F32)]
        + [pltpu.VMEM((N_RNN_BLOCKS, tc, RNN_BLOCK), F32)] * 3
        + [pltpu.VMEM((1, D_RNN), F32)],
        compiler_params=_cparams(2),
        name="rglru",
    )(z, z, cw, cb, wax, ba, bx, lam)


def _rglru_step_kernel(xr_ref, yg_ref, c0_ref, c1_ref, c2_ref, h0_ref, cw_ref, cb_ref, wax_ref, ba_ref,
                       bx_ref, lam_ref, aout_ref, hout_ref):
    taps = (c0_ref, c1_ref, c2_ref)
    for n in range(N_RNN_BLOCKS):
        cs = slice(n * RNN_BLOCK, (n + 1) * RNN_BLOCK)
        xc = cb_ref[:, cs]
        for j in range(CONV_WIDTH - 1):
            xc = xc + taps[j][:, cs] * cw_ref[j:j + 1, cs]
        xc = xc + xr_ref[:, cs].astype(F32) * cw_ref[CONV_WIDTH - 1:CONV_WIDTH, cs]
        a, b = _lru_gates_block(xc, n, wax_ref, ba_ref, bx_ref, lam_ref)
        h = a * h0_ref[:, cs] + b
        hout_ref[:, cs] = h
        aout_ref[:, cs] = (h * jax.nn.gelu(yg_ref[:, cs].astype(F32))).astype(BF16)


def _rglru_step(z, conv_taps, h0, cw, cb, wax, ba, bx, lam):
    n = z.shape[0]
    full = lambda shape: pl.BlockSpec(shape, lambda i: (0,) * len(shape))
    return pl.pallas_call(
        _rglru_step_kernel,
        grid=(1,),
        in_specs=[
            pl.BlockSpec((n, D_RNN), lambda i: (0, COL_XR // D_RNN)),
            pl.BlockSpec((n, D_RNN), lambda i: (0, COL_YG // D_RNN)),
            full((n, D_RNN)), full((n, D_RNN)), full((n, D_RNN)), full((n, D_RNN)),
            full((CONV_WIDTH, D_RNN)), full((1, D_RNN)),
            full((N_RNN_BLOCKS, RNN_BLOCK, 2 * RNN_BLOCK)),
            full((1, D_RNN)), full((1, D_RNN)), full((1, D_RNN)),
        ],
        out_specs=[full((n, D_RNN)), full((n, D_RNN))],
        out_shape=[jax.ShapeDtypeStruct((n, D_RNN), BF16), jax.ShapeDtypeStruct((n, D_RNN), F32)],
        compiler_params=_cparams(1),
        name="rglru_step",
    )(z, z, *conv_taps, h0, cw, cb, wax, ba, bx, lam)


def _t5_causal_buckets(dist):
    max_exact = N_BUCKETS // 2
    d = np.maximum(dist, 1).astype(np.float32)
    large = max_exact + (np.log(d / max_exact) / np.log(MAX_DISTANCE / max_exact)
                         * (N_BUCKETS - max_exact)).astype(np.int32)
    large = np.minimum(large, N_BUCKETS - 1)
    return np.where(dist < max_exact, dist, large).astype(np.int32)


def _group_bias(rel_bias, gi, window, dilation):
    buckets = _t5_causal_buckets(np.arange(window // dilation + 1) * dilation)
    heads = slice(gi * HEADS_PER_GROUP, (gi + 1) * HEADS_PER_GROUP)
    return rel_bias[buckets][:, heads].T.astype(F32)


def _band_bias(bias):
    nk = bias.shape[1]
    dist = np.arange(QUERY_BLOCK)[:, None] + (nk - 1) - np.arange(2 * QUERY_BLOCK)[None, :]
    valid = (dist >= 0) & (dist < nk)
    return jnp.where(valid[None], bias[:, np.clip(dist, 0, nk - 1)], MASK_VALUE)


def _attn_kernel(q_ref, k_ref, v_ref, bias_ref, o_ref, lse_ref, q_scr, kc_scr, vc_scr, o_scr, l_scr, *, d, sup):
    sb = pl.program_id(2)
    win = QUERY_BLOCK * d

    @pl.when(sb == 0)
    def _():
        kc_scr[0:win, :] = jnp.zeros((win, HEAD_DIM), F32)
        vc_scr[0:win, :] = jnp.zeros((win, HEAD_DIM), F32)

    q_scr[...] = q_ref[...].astype(F32)
    kc_scr[win:win + sup, :] = k_ref[...].astype(F32)
    vc_scr[win:win + sup, :] = v_ref[...].astype(F32)
    bias = bias_ref[...]
    col = lax.broadcasted_iota(jnp.int32, (QUERY_BLOCK, 2 * QUERY_BLOCK), 1)

    def rows(start, size):
        return pl.ds(start, size) if d == 1 else pl.ds(start, size, stride=d)

    for r in range(d):
        for qb in range(sup // win):
            start = r + win * qb
            qu = q_scr[rows(start, QUERY_BLOCK), :].astype(BF16)
            ku = kc_scr[rows(start, 2 * QUERY_BLOCK), :].astype(BF16)
            vu = vc_scr[rows(start, 2 * QUERY_BLOCK), :].astype(BF16)
            s = lax.dot_general(qu, ku, (((1,), (1,)), ((), ())), preferred_element_type=F32) * ATTN_SCALE + bias
            if qb == 0:
                s = jnp.where(jnp.logical_or(sb > 0, col >= QUERY_BLOCK), s, MASK_VALUE)
            m = jnp.max(s, axis=-1, keepdims=True)
            e = jnp.exp(s - m)
            den = jnp.sum(e, axis=-1, keepdims=True)
            o_scr[rows(start, QUERY_BLOCK), :] = jnp.dot((e / den).astype(BF16), vu, preferred_element_type=F32)
            l_scr[rows(start, QUERY_BLOCK), :] = jnp.broadcast_to(m + jnp.log(den), (QUERY_BLOCK, HEAD_DIM))

    o_ref[...] = o_scr[...].astype(BF16)
    lse_ref[...] = l_scr[...]
    kc_scr[0:win, :] = kc_scr[sup:sup + win, :]
    vc_scr[0:win, :] = vc_scr[sup:sup + win, :]


def _attn_prompt(z, band, gi, batch, seq):
    d = GROUPS[gi][1]
    sup = ATTN_SUPER
    win = QUERY_BLOCK * d
    nsb = seq // sup
    qb0 = COL_Q // HEAD_DIM + gi * HEADS_PER_GROUP
    kb0 = COL_K // HEAD_DIM + gi * HEADS_PER_GROUP
    vb0 = COL_V // HEAD_DIM + gi * HEADS_PER_GROUP
    blk = lambda c0: pl.BlockSpec((sup, HEAD_DIM), lambda b, h, s: (b * nsb + s, c0 + h))
    out = pl.BlockSpec((sup, HEAD_DIM), lambda b, h, s: (b * nsb + s, h))
    return pl.pallas_call(
        functools.partial(_attn_kernel, d=d, sup=sup),
        grid=(batch, HEADS_PER_GROUP, nsb),
        in_specs=[blk(qb0), blk(kb0), blk(vb0),
                  pl.BlockSpec((None, QUERY_BLOCK, 2 * QUERY_BLOCK), lambda b, h, s: (h, 0, 0))],
        out_specs=[out, out],
        out_shape=[jax.ShapeDtypeStruct((batch * seq, D_ATTN_OUT), BF16),
                   jax.ShapeDtypeStruct((batch * seq, D_ATTN_OUT), F32)],
        scratch_shapes=[pltpu.VMEM((sup, HEAD_DIM), F32), pltpu.VMEM((win + sup, HEAD_DIM), F32),
                        pltpu.VMEM((win + sup, HEAD_DIM), F32), pltpu.VMEM((sup, HEAD_DIM), F32),
                        pltpu.VMEM((sup, HEAD_DIM), F32)],
        compiler_params=_cparams(3),
        name=f"attn_g{gi}",
    )(z, z, z, band)


def _attn_step_kernel(q_ref, kn_ref, vn_ref, c0_ref, c1_ref, c2_ref, bc_ref, bn_ref, o_ref, lse_ref, *, bb):
    caches = (c0_ref, c1_ref, c2_ref)
    for b in range(bb):
        for g in range(len(GROUPS)):
            hs = slice(g * HEADS_PER_GROUP, (g + 1) * HEADS_PER_GROUP)
            qg = _round_bf16(q_ref[b, hs, :])
            kc = _round_bf16(caches[g][b, :, 0])
            vc = _round_bf16(caches[g][b, :, 1])
            kn = _round_bf16(kn_ref[b, hs, :])
            vn = _round_bf16(vn_ref[b, hs, :])
            sc = jnp.sum(kc * qg[None], axis=-1, keepdims=True) * ATTN_SCALE + bc_ref[g]
            sn = jnp.sum(kn * qg, axis=-1, keepdims=True) * ATTN_SCALE + bn_ref[g]
            m = jnp.maximum(jnp.max(sc, axis=0), sn)
            ec = jnp.exp(sc - m[None])
            en = jnp.exp(sn - m)
            den = jnp.sum(ec, axis=0) + en
            o = jnp.sum(_round_bf16(ec / den[None]) * vc, axis=0) + _round_bf16(en / den) * vn
            o_ref[g, b] = o
            lse_ref[g, b] = jnp.broadcast_to(m + jnp.log(den), (HEADS_PER_GROUP, HEAD_DIM))


def _attn_step(q, kn, vn, caches, bias_c, bias_n, bb):
    n = q.shape[0]
    ng = len(GROUPS)
    tok = pl.BlockSpec((bb, N_ATTN_HEADS, HEAD_DIM), lambda i: (i, 0, 0))
    cache = pl.BlockSpec((bb, QUERY_BLOCK, None, 2, HEADS_PER_GROUP, HEAD_DIM), lambda i: (i, 0, 0, 0, 0, 0))
    out = pl.BlockSpec((ng, bb, HEADS_PER_GROUP, HEAD_DIM), lambda i: (0, i, 0, 0))
    return pl.pallas_call(
        functools.partial(_attn_step_kernel, bb=bb),
        grid=(n // bb,),
        in_specs=[tok, tok, tok, cache, cache, cache,
                  pl.BlockSpec((ng, QUERY_BLOCK, HEADS_PER_GROUP, 1), lambda i: (0, 0, 0, 0)),
                  pl.BlockSpec((ng, HEADS_PER_GROUP, 1), lambda i: (0, 0, 0))],
        out_specs=[out, out],
        out_shape=[jax.ShapeDtypeStruct((ng, n, HEADS_PER_GROUP, HEAD_DIM), F32)] * 2,
        compiler_params=_cparams(1),
        name="attn_step",
    )(q, kn, vn, *caches, bias_c, bias_n)


def _mix_kernel(a_ref, o0_ref, o1_ref, o2_ref, l0_ref, l1_ref, l2_ref, ga_ref, gb_ref, x_ref,
                wr_ref, wa_ref, wo_ref, gf_ref, wrt_ref, brt_ref, cnt0_ref,
                x1_ref, xn_ref, gate_ref, dest_ref, cnt_ref, cnt_scr, *, cap):
    tm = x_ref.shape[0]

    @pl.when(pl.program_id(0) == 0)
    def _():
        cnt_scr[...] = cnt0_ref[...].astype(F32)

    l0, l1, l2 = l0_ref[...], l1_ref[...], l2_ref[...]
    mx = jnp.maximum(jnp.maximum(l0, l1), l2)
    e0, e1, e2 = jnp.exp(l0 - mx), jnp.exp(l1 - mx), jnp.exp(l2 - mx)
    den = e0 + e1 + e2
    b_out = (_round_bf16(e0 / den) * _round_bf16(o0_ref[...]) + _round_bf16(e1 / den) * _round_bf16(o1_ref[...])
             + _round_bf16(e2 / den) * _round_bf16(o2_ref[...]))
    rnn = jnp.dot(a_ref[...], wr_ref[...], preferred_element_type=F32)
    att = jnp.dot(b_out.astype(BF16), wa_ref[...], preferred_element_type=F32)
    merged = jax.nn.sigmoid(ga_ref[...].astype(F32)) * rnn + jax.nn.sigmoid(gb_ref[...].astype(F32)) * att
    x1 = x_ref[...] + jnp.dot(merged.astype(BF16), wo_ref[...], preferred_element_type=F32)
    x1_ref[...] = x1
    xn = _rms(x1, gf_ref[...])
    xn_ref[...] = xn

    logits = jnp.dot(xn.astype(BF16), wrt_ref[...], preferred_element_type=F32) + brt_ref[...]
    lane = lax.broadcasted_iota(jnp.int32, (tm, LANES), 1)
    cur = jnp.where(lane < N_EXPERTS, logits, -jnp.inf)
    vals, idxs = [], []
    for _ in range(TOP_K):
        mk = jnp.max(cur, axis=-1, keepdims=True)
        ik = jnp.min(jnp.where(cur == mk, lane, LANES), axis=-1, keepdims=True)
        vals.append(mk)
        idxs.append(ik)
        cur = jnp.where(lane == ik, -jnp.inf, cur)
    es = [jnp.exp(v - vals[0]) for v in vals]
    den = es[0] + es[1] + es[2] + es[3]
    chosen = jnp.zeros((tm, LANES), F32)
    for k in range(TOP_K):
        chosen = chosen + jnp.where(lane == idxs[k], 1.0, 0.0)
    tri = jnp.where(lax.broadcasted_iota(jnp.int32, (tm, tm), 1) <= lax.broadcasted_iota(jnp.int32, (tm, tm), 0),
                    1.0, 0.0).astype(BF16)
    incl = jnp.dot(tri, chosen.astype(BF16), preferred_element_type=F32)
    rank = cnt_scr[...] + incl - chosen
    gate_out = jnp.zeros((tm, LANES), F32)
    dest_out = jnp.zeros((tm, LANES), jnp.int32)
    for k in range(TOP_K):
        rank_k = jnp.sum(jnp.where(lane == idxs[k], rank, 0.0), axis=-1, keepdims=True).astype(jnp.int32)
        gate_out = jnp.where(lane == k, es[k] / den, gate_out)
        dest_out = jnp.where(lane == k, idxs[k] * cap + rank_k, dest_out)
    gate_ref[...] = gate_out
    dest_ref[...] = dest_out
    cnt_scr[...] = cnt_scr[...] + incl[tm - 1:tm, :]
    cnt_ref[...] = cnt_scr[...].astype(jnp.int32)


def _mix(a_out, os_, ls_, z, x2d, wr, wa, wo, gf, wrt, brt, cnt0, tm, cap):
    n = x2d.shape[0]
    row = lambda w, c=0: pl.BlockSpec((tm, w), lambda i: (i, c))
    full = lambda shape: pl.BlockSpec(shape, lambda i: (0,) * len(shape))
    return pl.pallas_call(
        functools.partial(_mix_kernel, cap=cap),
        grid=(n // tm,),
        in_specs=[row(D_RNN), row(D_ATTN_OUT), row(D_ATTN_OUT), row(D_ATTN_OUT),
                  row(D_ATTN_OUT), row(D_ATTN_OUT), row(D_ATTN_OUT),
                  row(D_MODEL, COL_GA // D_MODEL), row(D_MODEL, COL_GB // D_MODEL), row(D_MODEL),
                  full((D_RNN, D_MODEL)), full((D_ATTN_OUT, D_MODEL)), full((D_MODEL, D_MODEL)),
                  full((1, D_MODEL)), full((D_MODEL, LANES)), full((1, LANES)), full((1, LANES))],
        out_specs=[row(D_MODEL), row(D_MODEL), row(LANES), row(LANES), full((1, LANES))],
        out_shape=[jax.ShapeDtypeStruct((n, D_MODEL), F32), jax.ShapeDtypeStruct((n, D_MODEL), F32),
                   jax.ShapeDtypeStruct((n, LANES), F32), jax.ShapeDtypeStruct((n, LANES), jnp.int32),
                   jax.ShapeDtypeStruct((1, LANES), jnp.int32)],
        scratch_shapes=[pltpu.VMEM((1, LANES), F32)],
        compiler_params=_cparams(1),
        name="mix",
    )(a_out, *os_, *ls_, z, z, x2d, wr, wa, wo, gf, wrt, brt, cnt0)


def _row_copy(src_ref, src_row, dst_ref, dst_row, sem):
    return pltpu.make_async_copy(src_ref.at[pl.ds(src_row, 1)], dst_ref.at[pl.ds(dst_row, 1)], sem)


def _dispatch_kernel(dest_ref, xn_ref, *rest, tm):
    xb_ref, sem = rest[-2], rest[-1]
    base = pl.program_id(0) * tm

    def copies(r):
        return [_row_copy(xn_ref, base + r, xb_ref, dest_ref[TOP_K * r + k], sem) for k in range(TOP_K)]

    def issue(r, carry):
        for c in copies(r):
            c.start()
        return carry

    def drain(r, carry):
        for c in copies(r):
            c.wait()
        return carry

    lax.fori_loop(0, tm, issue, 0)
    lax.fori_loop(0, tm, drain, 0)


def _dispatch(dest_flat, xn, xb, tm, rows):
    n = xn.shape[0]
    any_spec = pl.BlockSpec(memory_space=pl.ANY)
    in_specs = [pl.BlockSpec((TOP_K * tm,), lambda i: (i,), memory_space=pltpu.SMEM), any_spec]
    args = [dest_flat, xn]
    aliases = {}
    if xb is not None:
        in_specs.append(any_spec)
        args.append(xb)
        aliases = {2: 0}
    return pl.pallas_call(
        functools.partial(_dispatch_kernel, tm=tm),
        grid=(n // tm,),
        in_specs=in_specs,
        out_specs=any_spec,
        out_shape=jax.ShapeDtypeStruct((rows, D_MODEL), F32),
        scratch_shapes=[pltpu.SemaphoreType.DMA],
        input_output_aliases=aliases,
        compiler_params=_cparams(1),
        name="dispatch",
    )(*args)


def _moe_kernel(be_ref, br_ref, nu_ref, xb_ref, wgu_ref, bgu_ref, wd_ref, bd_ref, y_ref, wgu_bf, wd_bf):
    i = pl.program_id(0)
    changed = jnp.logical_or(i == 0, be_ref[i] != be_ref[jnp.maximum(i - 1, 0)])

    @pl.when(changed)
    def _():
        wgu_bf[...] = wgu_ref[...].astype(BF16)
        wd_bf[...] = wd_ref[...].astype(BF16)

    @pl.when(i < nu_ref[0])
    def _():
        gu = jnp.dot(xb_ref[...].astype(BF16), wgu_bf[...], preferred_element_type=F32) + bgu_ref[...]
        g = jnp.minimum(gu[:, :D_EXPERT], SWIGLU_LIMIT)
        u = jnp.clip(gu[:, D_EXPERT:], -SWIGLU_LIMIT, SWIGLU_LIMIT)
        act = (u + 1.0) * (g * jax.nn.sigmoid(SWIGLU_ALPHA * g))
        y_ref[...] = jnp.dot(act.astype(BF16), wd_bf[...], preferred_element_type=F32) + bd_ref[...]


def _moe(block_e, block_row, n_used, xb, wgu, bgu, wd, bd):
    eb = EXPERT_ROWS
    grid_spec = pltpu.PrefetchScalarGridSpec(
        num_scalar_prefetch=3,
        grid=(block_e.shape[0],),
        in_specs=[
            pl.BlockSpec((eb, D_MODEL), lambda i, be, br, nu: (br[i], 0)),
            pl.BlockSpec((None, D_MODEL, 2 * D_EXPERT), lambda i, be, br, nu: (be[i], 0, 0)),
            pl.BlockSpec((None, 1, 2 * D_EXPERT), lambda i, be, br, nu: (be[i], 0, 0)),
            pl.BlockSpec((None, D_EXPERT, D_MODEL), lambda i, be, br, nu: (be[i], 0, 0)),
            pl.BlockSpec((None, 1, D_MODEL), lambda i, be, br, nu: (be[i], 0, 0)),
        ],
        out_specs=pl.BlockSpec((eb, D_MODEL), lambda i, be, br, nu: (br[i], 0)),
        scratch_shapes=[pltpu.VMEM((D_MODEL, 2 * D_EXPERT), BF16), pltpu.VMEM((D_EXPERT, D_MODEL), BF16)],
    )
    return pl.pallas_call(
        _moe_kernel,
        grid_spec=grid_spec,
        out_shape=jax.ShapeDtypeStruct(xb.shape, F32),
        compiler_params=_cparams(1),
        name="moe",
    )(block_e, block_row, n_used, xb, wgu, bgu, wd, bd)


def _expert_blocks(counts, cap, n_assign):
    eb = EXPERT_ROWS
    n_grid = -(-n_assign // eb) + N_EXPERTS
    nblk = (counts + eb - 1) // eb
    blk_end = jnp.cumsum(nblk)
    n_used = blk_end[-1:].astype(jnp.int32)
    i = jnp.minimum(jnp.arange(n_grid, dtype=jnp.int32), n_used[0] - 1)
    block_e = jnp.searchsorted(blk_end, i, side='right').astype(jnp.int32)
    block_row = block_e * (cap // eb) + i - (blk_end - nblk)[block_e]
    return block_e, block_row.astype(jnp.int32), n_used


def _ple_kernel(dest_ref, x1_ref, gate_ref, p_ref, gp_ref, wg_ref, wp_ref, yb_ref, y_ref, ybuf, sem):
    tm = x1_ref.shape[0]

    def copies(r):
        return [_row_copy(yb_ref, dest_ref[TOP_K * r + k], ybuf.at[k], r, sem) for k in range(TOP_K)]

    def issue(r, carry):
        for c in copies(r):
            c.start()
        return carry

    def drain(r, carry):
        for c in copies(r):
            c.wait()
        return carry

    lax.fori_loop(0, tm, issue, 0)
    lax.fori_loop(0, tm, drain, 0)

    gates = gate_ref[...]
    x2 = x1_ref[...]
    for k in range(TOP_K):
        x2 = x2 + ybuf[k] * gates[:, k:k + 1]
    gate = jax.nn.sigmoid(jnp.dot(_rms(x2, gp_ref[...]).astype(BF16), wg_ref[...], preferred_element_type=F32))
    proj = jnp.dot(p_ref[...].astype(BF16), wp_ref[...], preferred_element_type=F32)
    y_ref[...] = x2 + gate * proj


def _ple(dest_flat, x1, gates, p2d, gp, wg, wp, yb, tm):
    n = x1.shape[0]
    row = lambda w: pl.BlockSpec((tm, w), lambda i: (i, 0))
    full = lambda shape: pl.BlockSpec(shape, lambda i: (0,) * len(shape))
    return pl.pallas_call(
        _ple_kernel,
        grid=(n // tm,),
        in_specs=[pl.BlockSpec((TOP_K * tm,), lambda i: (i,), memory_space=pltpu.SMEM),
                  row(D_MODEL), row(LANES), row(D_PLE), full((1, D_MODEL)),
                  full((D_MODEL, D_MODEL)), full((D_PLE, D_MODEL)), pl.BlockSpec(memory_space=pl.ANY)],
        out_specs=row(D_MODEL),
        out_shape=jax.ShapeDtypeStruct((n, D_MODEL), F32),
        scratch_shapes=[pltpu.VMEM((TOP_K, tm, D_MODEL), F32), pltpu.SemaphoreType.DMA],
        compiler_params=_cparams(1),
        name="ple",
    )(dest_flat, x1, gates, p2d, gp, wg, wp, yb)


def kernel(x_prompt, x_sample, state_conv, state_h, cache_kv_w128, cache_kv_w512, cache_kv_w2048, p_prompt,
           p_sample, norm_mix_g, w_in, conv_w, conv_b, w_rg_a, b_rg_a, w_rg_x, b_rg_x, lru_lambda, q_norm_g,
           k_norm_g, rel_bias, w_br_rnn, w_br_attn, w_out, norm_ffn_g, w_router, b_router, w_gate_up, b_gate_up,
           w_down, b_down, norm_ple_g, w_ple_gate, w_ple_proj):
    bp, seq, _ = x_prompt.shape
    bs = x_sample.shape[0]
    n_p = bp * seq
    li = 0
    row = lambda v: v.reshape(1, -1).astype(F32)

    w = w_in[li]
    o_q, o_ga = 2 * D_RNN, 2 * D_RNN + 3 * D_QKV
    w_perm = jnp.concatenate([w[:, :o_q], w[:, o_ga:], w[:, o_q:o_ga]], axis=1).astype(BF16)
    colgain = jnp.concatenate([jnp.ones((COL_Q,), F32), jnp.tile(q_norm_g[li], N_ATTN_HEADS),
                               jnp.tile(k_norm_g[li], N_ATTN_HEADS), jnp.ones((D_QKV,), F32)]).reshape(1, D_IN)
    wax = jnp.concatenate([w_rg_a[li], w_rg_x[li]], axis=-1).astype(BF16)
    lru = (conv_w[li].astype(F32), row(conv_b[li]), wax, row(b_rg_a[li]), row(b_rg_x[li]), row(lru_lambda[li]))
    wr, wa, wo = w_br_rnn[li].astype(BF16), w_br_attn[li].astype(BF16), w_out[li].astype(BF16)
    wrt = jnp.pad(w_router[li].astype(BF16), ((0, 0), (0, LANES - N_EXPERTS)))
    brt = jnp.pad(row(b_router[li]), ((0, 0), (0, LANES - N_EXPERTS)))
    gf = row(norm_ffn_g[li])
    biases = [_group_bias(rel_bias, gi, wnd, dil) for gi, (wnd, dil) in enumerate(GROUPS)]

    xp = x_prompt.reshape(n_p, D_MODEL)
    zp = _inproj(xp, row(norm_mix_g[li]), w_perm, colgain, tm=512, out_dtype=BF16)
    a_p, h_p = _rglru_prompt(zp, bp, seq, 512, *lru)
    os_p, ls_p = [], []
    for gi in range(len(GROUPS)):
        o, l = _attn_prompt(zp, _band_bias(biases[gi]), gi, bp, seq)
        os_p.append(o)
        ls_p.append(l)
    cap = -(-(n_p + bs) // EXPERT_ROWS) * EXPERT_ROWS
    x1_p, xn_p, gate_p, dest_p, cnt_p = _mix(a_p, os_p, ls_p, zp, xp, wr, wa, wo, gf, wrt, brt,
                                              jnp.zeros((1, LANES), jnp.int32), tm=256, cap=cap)

    xs = x_sample.reshape(bs, D_MODEL)
    zs = _inproj(xs, row(norm_mix_g[li]), w_perm, colgain, tm=bs, out_dtype=F32)
    sc = state_conv[li]
    a_s, h_s = _rglru_step(zs, [sc[:, j] for j in range(CONV_WIDTH - 1)], state_h[li], *lru)
    heads = lambda c0: zs[:, c0:c0 + D_QKV].astype(F32).reshape(bs, N_ATTN_HEADS, HEAD_DIM)
    caches = [c[li].reshape(bs, QUERY_BLOCK, dil, 2, HEADS_PER_GROUP, HEAD_DIM)
              for c, (_, dil) in zip((cache_kv_w128, cache_kv_w512, cache_kv_w2048), GROUPS)]
    bias_c = jnp.stack([b[:, :0:-1].T for b in biases])[..., None]
    bias_n = jnp.stack([b[:, 0] for b in biases])[..., None]
    o_s, l_s = _attn_step(heads(COL_Q), heads(COL_K), heads(COL_V), caches, bias_c, bias_n, bb=4)
    os_s = [o_s[g].reshape(bs, D_ATTN_OUT) for g in range(len(GROUPS))]
    ls_s = [l_s[g].reshape(bs, D_ATTN_OUT) for g in range(len(GROUPS))]
    x1_s, xn_s, gate_s, dest_s, cnt_s = _mix(a_s, os_s, ls_s, zs, xs, wr, wa, wo, gf, wrt, brt, cnt_p, tm=bs, cap=cap)

    dflat_p = dest_p[:, :TOP_K].reshape(-1)
    dflat_s = dest_s[:, :TOP_K].reshape(-1)
    xb = _dispatch(dflat_p, xn_p, None, 256, N_EXPERTS * cap)
    xb = _dispatch(dflat_s, xn_s, xb, bs, N_EXPERTS * cap)
    block_e, block_row, n_used = _expert_blocks(cnt_s[0, :N_EXPERTS], cap, (n_p + bs) * TOP_K)
    yb = _moe(block_e, block_row, n_used, xb, w_gate_up[li], b_gate_up[li][:, None, :],
              w_down[li], b_down[li][:, None, :])

    gp, wg, wp = row(norm_ple_g[li]), w_ple_gate[li].astype(BF16), w_ple_proj[li].astype(BF16)
    y_p = _ple(dflat_p, x1_p, gate_p, p_prompt[li].reshape(n_p, D_PLE), gp, wg, wp, yb, tm=256)
    y_s = _ple(dflat_s, x1_s, gate_s, p_sample[li].reshape(bs, D_PLE), gp, wg, wp, yb, tm=bs)

    z3 = zp.reshape(bp, seq, D_IN)
    zs3 = zs.reshape(bs, 1, D_IN)

    def new_kv(zz, gi, keep):
        t = zz.shape[1]
        c = gi * D_ATTN_OUT
        kk = zz[:, t - keep:, COL_K + c:COL_K + c + D_ATTN_OUT].astype(F32)
        vv = zz[:, t - keep:, COL_V + c:COL_V + c + D_ATTN_OUT].astype(F32)
        shape = (zz.shape[0], keep, HEADS_PER_GROUP, HEAD_DIM)
        return jnp.stack([kk.reshape(shape), vv.reshape(shape)], axis=2)[None]

    conv_p = z3[:, seq - (CONV_WIDTH - 1):, COL_XR:COL_XR + D_RNN].astype(F32)[None]
    conv_s = jnp.concatenate([sc[:, 1:], zs3[:, :, COL_XR:COL_XR + D_RNN].astype(F32)], axis=1)[None]
    outs = [y_p.reshape(bp, seq, D_MODEL), y_s.reshape(bs, 1, D_MODEL), conv_p, conv_s,
            h_p.reshape(1, bp, D_RNN), h_s.reshape(1, bs, D_RNN)]
    for gi, (wnd, _) in enumerate(GROUPS):
        outs.append(new_kv(z3, gi, min(wnd, seq)))
        outs.append(new_kv(zs3, gi, 1))
    return tuple(outs)
```

```python
import functools
import math

import numpy as np
import jax
import jax.numpy as jnp
from jax import lax
from jax.experimental import pallas as pl
from jax.experimental.pallas import tpu as pltpu

F32 = jnp.float32
BF16 = jnp.bfloat16

D_MODEL = 1024
D_RNN = 1024
N_RNN_BLOCKS = 8
RNN_BLOCK = D_RNN // N_RNN_BLOCKS
CONV_WIDTH = 4
LRU_C = 8.0
GROUPS = ((128, 1), (512, 4), (2048, 16))
HEADS_PER_GROUP = 4
HEAD_DIM = 128
N_ATTN_HEADS = HEADS_PER_GROUP * len(GROUPS)
D_QKV = N_ATTN_HEADS * HEAD_DIM
D_ATTN_OUT = HEADS_PER_GROUP * HEAD_DIM
N_BUCKETS = 32
MAX_DISTANCE = 2048
QUERY_BLOCK = 128
D_IN = 2 * D_RNN + 3 * D_QKV + 2 * D_MODEL
N_EXPERTS = 32
TOP_K = 4
D_EXPERT = 1024
SWIGLU_ALPHA = 1.702
SWIGLU_LIMIT = 7.0
D_PLE = 256
EPS = 1e-6
ATTN_SCALE = HEAD_DIM ** -0.5
MASK_VALUE = -1e30

LANES = 128
ROW_TILE = D_MODEL // LANES
COL_XR, COL_YG, COL_GA, COL_GB = 0, D_RNN, 2 * D_RNN, 2 * D_RNN + D_MODEL
COL_Q = 2 * D_RNN + 2 * D_MODEL
COL_K = COL_Q + D_QKV
COL_V = COL_K + D_QKV
INPROJ_TN = 512
INPROJ_SUB = 512
ATTN_SUPER = 2048
EXPERT_ROWS = 256
VMEM_LIMIT = 56 * 1024 * 1024


def _cparams(n_axes):
    return pltpu.CompilerParams(dimension_semantics=("arbitrary",) * n_axes, vmem_limit_bytes=VMEM_LIMIT)


def _rms(x, g):
    ms = jnp.mean(x * x, axis=-1, keepdims=True)
    return (x * lax.rsqrt(ms + EPS)) * g


def _round_bf16(x):
    return x.astype(BF16).astype(F32)


def _load_row_tiles(ref, n_rows):
    return jnp.concatenate([ref[pl.ds(c, n_rows, stride=ROW_TILE), :] for c in range(ROW_TILE)], axis=1)


def _store_row_tiles(ref, value):
    n_rows = value.shape[0]
    for c in range(ROW_TILE):
        ref[pl.ds(c, n_rows, stride=ROW_TILE), :] = value[:, c * LANES:(c + 1) * LANES]


def _row_tile_copy(src_ref, src_row, dst_ref, dst_row, sem):
    src = src_ref.at[pl.ds(pl.multiple_of(src_row * ROW_TILE, ROW_TILE), ROW_TILE)]
    dst = dst_ref.at[pl.ds(pl.multiple_of(dst_row * ROW_TILE, ROW_TILE), ROW_TILE)]
    return pltpu.make_async_copy(src, dst, sem)


def _inproj_kernel(x_ref, g_ref, w_ref, cg_ref, z_ref, u_scr, *, sub):
    j = pl.program_id(1)
    n_sub = x_ref.shape[0] // sub

    @pl.when(j == 0)
    def _():
        for s in range(n_sub):
            rows = slice(s * sub, (s + 1) * sub)
            u_scr[rows, :] = _rms(x_ref[rows, :], g_ref[...]).astype(BF16)

    is_qk = jnp.logical_and(j >= COL_Q // INPROJ_TN, j < COL_V // INPROJ_TN)
    for s in range(n_sub):
        rows = slice(s * sub, (s + 1) * sub)
        acc = jnp.dot(u_scr[rows, :], w_ref[...], preferred_element_type=F32)
        for h in range(INPROJ_TN // HEAD_DIM):
            cs = slice(h * HEAD_DIM, (h + 1) * HEAD_DIM)
            a = acc[:, cs]
            z_ref[rows, cs] = jnp.where(is_qk, _rms(a, cg_ref[:, cs]), a).astype(z_ref.dtype)


def _inproj(x2d, g, w_perm, colgain, tm, out_dtype):
    n = x2d.shape[0]
    return pl.pallas_call(
        functools.partial(_inproj_kernel, sub=min(tm, INPROJ_SUB)),
        grid=(n // tm, D_IN // INPROJ_TN),
        in_specs=[
            pl.BlockSpec((tm, D_MODEL), lambda i, j: (i, 0)),
            pl.BlockSpec((1, D_MODEL), lambda i, j: (0, 0)),
            pl.BlockSpec((D_MODEL, INPROJ_TN), lambda i, j: (0, j)),
            pl.BlockSpec((1, INPROJ_TN), lambda i, j: (0, j)),
        ],
        out_specs=pl.BlockSpec((tm, INPROJ_TN), lambda i, j: (i, j)),
        out_shape=jax.ShapeDtypeStruct((n, D_IN), out_dtype),
        scratch_shapes=[pltpu.VMEM((tm, D_MODEL), BF16)],
        compiler_params=_cparams(2),
        name="inproj",
    )(x2d, g, w_perm, colgain)


GELU_C1 = 2.0 * math.sqrt(2.0 / math.pi)
GELU_C2 = GELU_C1 * 0.044715


def _gelu_tanh(x):
    return x / (1.0 + jnp.exp(x * (-GELU_C1 - GELU_C2 * (x * x))))


def _softplus(v):
    return jnp.maximum(v, 0.0) + jnp.log1p(jnp.exp(-jnp.abs(v)))


def _lru_gates_block(xc, n, wax_ref, ba_ref, bx_ref, lam_ref):
    cs = slice(n * RNN_BLOCK, (n + 1) * RNN_BLOCK)
    ri = jnp.dot(xc.astype(BF16), wax_ref[n], preferred_element_type=F32)
    r = jax.nn.sigmoid(ri[:, :RNN_BLOCK] + ba_ref[:, cs])
    i = jax.nn.sigmoid(ri[:, RNN_BLOCK:] + bx_ref[:, cs])
    log_a = (-LRU_C * r) * _softplus(-lam_ref[:, cs])
    a = jnp.exp(log_a)
    v = 1.0 - a * a
    b = jnp.where(v > 0.0, v * lax.rsqrt(v), 0.0) * (i * xc)
    return a, b


def _rglru_kernel(xr_ref, yg_ref, cw_ref, cb_ref, wax_ref, ba_ref, bx_ref, lam_ref,
                  aout_ref, hlast_ref, xe_scr, a_scr, b_scr, ac_scr, hcar_scr, *, tc):
    c = pl.program_id(1)
    seg = tc // 8

    @pl.when(c == 0)
    def _():
        xe_scr[0:8, :] = jnp.zeros((8, D_RNN), F32)
        hcar_scr[...] = jnp.zeros((1, D_RNN), F32)

    xe_scr[8:8 + tc, :] = xr_ref[...].astype(F32)
    for n in range(N_RNN_BLOCKS):
        cs = slice(n * RNN_BLOCK, (n + 1) * RNN_BLOCK)
        xc = cb_ref[:, cs]
        for j in range(CONV_WIDTH):
            xc = xc + xe_scr[pl.ds(8 - (CONV_WIDTH - 1) + j, tc), cs] * cw_ref[j:j + 1, cs]
        a, b = _lru_gates_block(xc, n, wax_ref, ba_ref, bx_ref, lam_ref)
        a_scr[n] = a
        b_scr[n] = b
    xe_scr[0:8, :] = xe_scr[tc:tc + 8, :]

    def step(j, carry):
        hs, accs = carry
        rows = pl.ds(j, 8, stride=seg)
        new_h, new_acc = [], []
        for n in range(N_RNN_BLOCKS):
            aj = a_scr[n, rows, :]
            h = aj * hs[n] + b_scr[n, rows, :]
            acc = accs[n] * aj
            b_scr[n, rows, :] = h
            ac_scr[n, rows, :] = acc
            new_h.append(h)
            new_acc.append(acc)
        return tuple(new_h), tuple(new_acc)

    init = (tuple(jnp.zeros((8, RNN_BLOCK), F32) for _ in range(N_RNN_BLOCKS)),
            tuple(jnp.ones((8, RNN_BLOCK), F32) for _ in range(N_RNN_BLOCKS)))
    h_end, a_end = lax.fori_loop(0, seg, step, init)

    for n in range(N_RNN_BLOCKS):
        cs = slice(n * RNN_BLOCK, (n + 1) * RNN_BLOCK)
        s = hcar_scr[:, cs]
        for k in range(8):
            rows = slice(k * seg, (k + 1) * seg)
            h = b_scr[n, rows, :] + ac_scr[n, rows, :] * s
            aout_ref[rows, cs] = (h * _gelu_tanh(yg_ref[rows, cs].astype(F32))).astype(BF16)
            s = a_end[n][k:k + 1, :] * s + h_end[n][k:k + 1, :]
        hcar_scr[:, cs] = s
        hlast_ref[:, cs] = s


def _rglru_prompt(z, batch, seq, tc, cw, cb, wax, ba, bx, lam):
    nc = seq // tc
    vec = lambda r: pl.BlockSpec((r, D_RNN), lambda b, c: (0, 0))
    return pl.pallas_call(
        functools.partial(_rglru_kernel, tc=tc),
        grid=(batch, nc),
        in_specs=[
            pl.BlockSpec((tc, D_RNN), lambda b, c: (b * nc + c, COL_XR // D_RNN)),
            pl.BlockSpec((tc, D_RNN), lambda b, c: (b * nc + c, COL_YG // D_RNN)),
            vec(CONV_WIDTH), vec(1),
            pl.BlockSpec((N_RNN_BLOCKS, RNN_BLOCK, 2 * RNN_BLOCK), lambda b, c: (0, 0, 0)),
            vec(1), vec(1), vec(1),
        ],
        out_specs=[
            pl.BlockSpec((tc, D_RNN), lambda b, c: (b * nc + c, 0)),
            pl.BlockSpec((None, 1, D_RNN), lambda b, c: (b, 0, 0)),
        ],
        out_shape=[jax.ShapeDtypeStruct((batch * seq, D_RNN), BF16),
                   jax.ShapeDtypeStruct((batch, 1, D_RNN), F32)],
        scratch_shapes=[pltpu.VMEM((tc + 8, D_RNN), F32)]
        + [pltpu.VMEM((N_RNN_BLOCKS, tc, RNN_BLOCK), F32)] * 3
        + [pltpu.VMEM((1, D_RNN), F32)],
        compiler_params=_cparams(2),
        name="rglru",
    )(z, z, cw, cb, wax, ba, bx, lam)


def _rglru_step_kernel(xr_ref, yg_ref, c0_ref, c1_ref, c2_ref, h0_ref, cw_ref, cb_ref, wax_ref, ba_ref,
                       bx_ref, lam_ref, aout_ref, hout_ref):
    taps = (c0_ref, c1_ref, c2_ref)
    for n in range(N_RNN_BLOCKS):
        cs = slice(n * RNN_BLOCK, (n + 1) * RNN_BLOCK)
        xc = cb_ref[:, cs]
        for j in range(CONV_WIDTH - 1):
            xc = xc + taps[j][:, cs] * cw_ref[j:j + 1, cs]
        xc = xc + xr_ref[:, cs].astype(F32) * cw_ref[CONV_WIDTH - 1:CONV_WIDTH, cs]
        a, b = _lru_gates_block(xc, n, wax_ref, ba_ref, bx_ref, lam_ref)
        h = a * h0_ref[:, cs] + b
        hout_ref[:, cs] = h
        aout_ref[:, cs] = (h * _gelu_tanh(yg_ref[:, cs].astype(F32))).astype(BF16)


def _rglru_step(z, conv_taps, h0, cw, cb, wax, ba, bx, lam):
    n = z.shape[0]
    full = lambda shape: pl.BlockSpec(shape, lambda i: (0,) * len(shape))
    return pl.pallas_call(
        _rglru_step_kernel,
        grid=(1,),
        in_specs=[
            pl.BlockSpec((n, D_RNN), lambda i: (0, COL_XR // D_RNN)),
            pl.BlockSpec((n, D_RNN), lambda i: (0, COL_YG // D_RNN)),
            full((n, D_RNN)), full((n, D_RNN)), full((n, D_RNN)), full((n, D_RNN)),
            full((CONV_WIDTH, D_RNN)), full((1, D_RNN)),
            full((N_RNN_BLOCKS, RNN_BLOCK, 2 * RNN_BLOCK)),
            full((1, D_RNN)), full((1, D_RNN)), full((1, D_RNN)),
        ],
        out_specs=[full((n, D_RNN)), full((n, D_RNN))],
        out_shape=[jax.ShapeDtypeStruct((n, D_RNN), BF16), jax.ShapeDtypeStruct((n, D_RNN), F32)],
        compiler_params=_cparams(1),
        name="rglru_step",
    )(z, z, *conv_taps, h0, cw, cb, wax, ba, bx, lam)


def _t5_causal_buckets(dist):
    max_exact = N_BUCKETS // 2
    d = np.maximum(dist, 1).astype(np.float32)
    large = max_exact + (np.log(d / max_exact) / np.log(MAX_DISTANCE / max_exact)
                         * (N_BUCKETS - max_exact)).astype(np.int32)
    large = np.minimum(large, N_BUCKETS - 1)
    return np.where(dist < max_exact, dist, large).astype(np.int32)


def _group_bias(rel_bias, gi, window, dilation):
    buckets = _t5_causal_buckets(np.arange(window // dilation + 1) * dilation)
    heads = slice(gi * HEADS_PER_GROUP, (gi + 1) * HEADS_PER_GROUP)
    return rel_bias[buckets][:, heads].T.astype(F32)


def _band_bias(bias):
    nh, nk = bias.shape
    qb, span = QUERY_BLOCK, 2 * QUERY_BLOCK
    assert nk == qb + 1
    w = jnp.concatenate([jnp.full((nh, qb - 1), MASK_VALUE, F32), bias[:, ::-1], jnp.full((nh, qb), MASK_VALUE, F32)],
                        axis=1)
    skew = jnp.tile(w, (1, qb + 1))[:, :qb * (w.shape[1] + 1)].reshape(nh, qb, w.shape[1] + 1)
    return skew[:, ::-1, :span]


def _attn_kernel(q_ref, k_ref, v_ref, bias_ref, o_ref, lse_ref, q_scr, kc_scr, vc_scr, o_scr, l_scr, *, d, sup):
    sb = pl.program_id(2)
    win = QUERY_BLOCK * d

    @pl.when(sb == 0)
    def _():
        kc_scr[0:win, :] = jnp.zeros((win, HEAD_DIM), F32)
        vc_scr[0:win, :] = jnp.zeros((win, HEAD_DIM), F32)

    q_scr[...] = q_ref[...].astype(F32)
    kc_scr[win:win + sup, :] = k_ref[...].astype(F32)
    vc_scr[win:win + sup, :] = v_ref[...].astype(F32)
    bias = bias_ref[...]
    col = lax.broadcasted_iota(jnp.int32, (QUERY_BLOCK, 2 * QUERY_BLOCK), 1)

    def rows(start, size):
        return pl.ds(start, size) if d == 1 else pl.ds(start, size, stride=d)

    for r in range(d):
        for qb in range(sup // win):
            start = r + win * qb
            qu = q_scr[rows(start, QUERY_BLOCK), :].astype(BF16)
            ku = kc_scr[rows(start, 2 * QUERY_BLOCK), :].astype(BF16)
            vu = vc_scr[rows(start, 2 * QUERY_BLOCK), :].astype(BF16)
            s = lax.dot_general(qu, ku, (((1,), (1,)), ((), ())), preferred_element_type=F32) * ATTN_SCALE + bias
            if qb == 0:
                s = jnp.where(jnp.logical_or(sb > 0, col >= QUERY_BLOCK), s, MASK_VALUE)
            m = jnp.max(s, axis=-1, keepdims=True)
            e = jnp.exp(s - m)
            den = jnp.sum(e, axis=-1, keepdims=True)
            o_scr[rows(start, QUERY_BLOCK), :] = jnp.dot((e / den).astype(BF16), vu, preferred_element_type=F32)
            l_scr[rows(start, QUERY_BLOCK), :] = jnp.broadcast_to(m + jnp.log(den), (QUERY_BLOCK, HEAD_DIM))

    o_ref[...] = o_scr[...].astype(BF16)
    lse_ref[...] = l_scr[...]
    kc_scr[0:win, :] = kc_scr[sup:sup + win, :]
    vc_scr[0:win, :] = vc_scr[sup:sup + win, :]


def _attn_prompt(z, band, gi, batch, seq):
    d = GROUPS[gi][1]
    sup = ATTN_SUPER
    win = QUERY_BLOCK * d
    nsb = seq // sup
    qb0 = COL_Q // HEAD_DIM + gi * HEADS_PER_GROUP
    kb0 = COL_K // HEAD_DIM + gi * HEADS_PER_GROUP
    vb0 = COL_V // HEAD_DIM + gi * HEADS_PER_GROUP
    blk = lambda c0: pl.BlockSpec((sup, HEAD_DIM), lambda b, h, s: (b * nsb + s, c0 + h))
    out = pl.BlockSpec((sup, HEAD_DIM), lambda b, h, s: (b * nsb + s, h))
    return pl.pallas_call(
        functools.partial(_attn_kernel, d=d, sup=sup),
        grid=(batch, HEADS_PER_GROUP, nsb),
        in_specs=[blk(qb0), blk(kb0), blk(vb0),
                  pl.BlockSpec((None, QUERY_BLOCK, 2 * QUERY_BLOCK), lambda b, h, s: (h, 0, 0))],
        out_specs=[out, out],
        out_shape=[jax.ShapeDtypeStruct((batch * seq, D_ATTN_OUT), BF16),
                   jax.ShapeDtypeStruct((batch * seq, D_ATTN_OUT), F32)],
        scratch_shapes=[pltpu.VMEM((sup, HEAD_DIM), F32), pltpu.VMEM((win + sup, HEAD_DIM), F32),
                        pltpu.VMEM((win + sup, HEAD_DIM), F32), pltpu.VMEM((sup, HEAD_DIM), F32),
                        pltpu.VMEM((sup, HEAD_DIM), F32)],
        compiler_params=_cparams(3),
        name=f"attn_g{gi}",
    )(z, z, z, band)


def _attn_step_kernel(q_ref, kn_ref, vn_ref, c0_ref, c1_ref, c2_ref, bc_ref, bn_ref, o_ref, lse_ref, *, bb):
    caches = (c0_ref, c1_ref, c2_ref)
    for b in range(bb):
        for g in range(len(GROUPS)):
            hs = slice(g * HEADS_PER_GROUP, (g + 1) * HEADS_PER_GROUP)
            qg = _round_bf16(q_ref[b, hs, :])
            kc = _round_bf16(caches[g][b, :, 0])
            vc = _round_bf16(caches[g][b, :, 1])
            kn = _round_bf16(kn_ref[b, hs, :])
            vn = _round_bf16(vn_ref[b, hs, :])
            sc = jnp.sum(kc * qg[None], axis=-1, keepdims=True) * ATTN_SCALE + bc_ref[g]
            sn = jnp.sum(kn * qg, axis=-1, keepdims=True) * ATTN_SCALE + bn_ref[g]
            m = jnp.maximum(jnp.max(sc, axis=0), sn)
            ec = jnp.exp(sc - m[None])
            en = jnp.exp(sn - m)
            den = jnp.sum(ec, axis=0) + en
            o = jnp.sum(_round_bf16(ec / den[None]) * vc, axis=0) + _round_bf16(en / den) * vn
            o_ref[g, b] = o
            lse_ref[g, b] = jnp.broadcast_to(m + jnp.log(den), (HEADS_PER_GROUP, HEAD_DIM))


def _attn_step(q, kn, vn, caches, bias_c, bias_n, bb):
    n = q.shape[0]
    ng = len(GROUPS)
    tok = pl.BlockSpec((bb, N_ATTN_HEADS, HEAD_DIM), lambda i: (i, 0, 0))
    cache = pl.BlockSpec((bb, QUERY_BLOCK, None, 2, HEADS_PER_GROUP, HEAD_DIM), lambda i: (i, 0, 0, 0, 0, 0))
    out = pl.BlockSpec((ng, bb, HEADS_PER_GROUP, HEAD_DIM), lambda i: (0, i, 0, 0))
    return pl.pallas_call(
        functools.partial(_attn_step_kernel, bb=bb),
        grid=(n // bb,),
        in_specs=[tok, tok, tok, cache, cache, cache,
                  pl.BlockSpec((ng, QUERY_BLOCK, HEADS_PER_GROUP, 1), lambda i: (0, 0, 0, 0)),
                  pl.BlockSpec((ng, HEADS_PER_GROUP, 1), lambda i: (0, 0, 0))],
        out_specs=[out, out],
        out_shape=[jax.ShapeDtypeStruct((ng, n, HEADS_PER_GROUP, HEAD_DIM), F32)] * 2,
        compiler_params=_cparams(1),
        name="attn_step",
    )(q, kn, vn, *caches, bias_c, bias_n)


def _mix_kernel(a_ref, o0_ref, o1_ref, o2_ref, l0_ref, l1_ref, l2_ref, ga_ref, gb_ref, x_ref,
                wr_ref, wa_ref, wo_ref, gf_ref, wrt_ref, brt_ref, cnt0_ref,
                x1_ref, xn_ref, gate_ref, dest_ref, cnt_ref, cnt_scr, *, cap):
    tm = x_ref.shape[0]

    @pl.when(pl.program_id(0) == 0)
    def _():
        cnt_scr[...] = cnt0_ref[...].astype(F32)

    l0, l1, l2 = l0_ref[...], l1_ref[...], l2_ref[...]
    mx = jnp.maximum(jnp.maximum(l0, l1), l2)
    e0, e1, e2 = jnp.exp(l0 - mx), jnp.exp(l1 - mx), jnp.exp(l2 - mx)
    den = e0 + e1 + e2
    b_out = (_round_bf16(e0 / den) * _round_bf16(o0_ref[...]) + _round_bf16(e1 / den) * _round_bf16(o1_ref[...])
             + _round_bf16(e2 / den) * _round_bf16(o2_ref[...]))
    rnn = jnp.dot(a_ref[...], wr_ref[...], preferred_element_type=F32)
    att = jnp.dot(b_out.astype(BF16), wa_ref[...], preferred_element_type=F32)
    merged = jax.nn.sigmoid(ga_ref[...].astype(F32)) * rnn + jax.nn.sigmoid(gb_ref[...].astype(F32)) * att
    x1 = x_ref[...] + jnp.dot(merged.astype(BF16), wo_ref[...], preferred_element_type=F32)
    x1_ref[...] = x1
    xn = _rms(x1, gf_ref[...])
    _store_row_tiles(xn_ref, xn)

    logits = jnp.dot(xn.astype(BF16), wrt_ref[...], preferred_element_type=F32) + brt_ref[...]
    lane = lax.broadcasted_iota(jnp.int32, (tm, LANES), 1)
    cur = jnp.where(lane < N_EXPERTS, logits, -jnp.inf)
    vals, idxs = [], []
    for _ in range(TOP_K):
        mk = jnp.max(cur, axis=-1, keepdims=True)
        ik = jnp.min(jnp.where(cur == mk, lane, LANES), axis=-1, keepdims=True)
        vals.append(mk)
        idxs.append(ik)
        cur = jnp.where(lane == ik, -jnp.inf, cur)
    es = [jnp.exp(v - vals[0]) for v in vals]
    den = es[0] + es[1] + es[2] + es[3]
    chosen = jnp.zeros((tm, LANES), F32)
    for k in range(TOP_K):
        chosen = chosen + jnp.where(lane == idxs[k], 1.0, 0.0)
    tri = jnp.where(lax.broadcasted_iota(jnp.int32, (tm, tm), 1) <= lax.broadcasted_iota(jnp.int32, (tm, tm), 0),
                    1.0, 0.0).astype(BF16)
    incl = jnp.dot(tri, chosen.astype(BF16), preferred_element_type=F32)
    rank = cnt_scr[...] + incl - chosen
    gate_out = jnp.zeros((tm, LANES), F32)
    dest_out = jnp.zeros((tm, LANES), jnp.int32)
    for k in range(TOP_K):
        rank_k = jnp.sum(jnp.where(lane == idxs[k], rank, 0.0), axis=-1, keepdims=True).astype(jnp.int32)
        gate_out = jnp.where(lane == k, es[k] / den, gate_out)
        dest_out = jnp.where(lane == k, idxs[k] * cap + rank_k, dest_out)
    gate_ref[...] = gate_out
    dest_ref[...] = dest_out
    cnt_scr[...] = cnt_scr[...] + incl[tm - 1:tm, :]
    cnt_ref[...] = cnt_scr[...].astype(jnp.int32)


def _mix(a_out, os_, ls_, z, x2d, wr, wa, wo, gf, wrt, brt, cnt0, tm, cap):
    n = x2d.shape[0]
    row = lambda w, c=0: pl.BlockSpec((tm, w), lambda i: (i, c))
    full = lambda shape: pl.BlockSpec(shape, lambda i: (0,) * len(shape))
    return pl.pallas_call(
        functools.partial(_mix_kernel, cap=cap),
        grid=(n // tm,),
        in_specs=[row(D_RNN), row(D_ATTN_OUT), row(D_ATTN_OUT), row(D_ATTN_OUT),
                  row(D_ATTN_OUT), row(D_ATTN_OUT), row(D_ATTN_OUT),
                  row(D_MODEL, COL_GA // D_MODEL), row(D_MODEL, COL_GB // D_MODEL), row(D_MODEL),
                  full((D_RNN, D_MODEL)), full((D_ATTN_OUT, D_MODEL)), full((D_MODEL, D_MODEL)),
                  full((1, D_MODEL)), full((D_MODEL, LANES)), full((1, LANES)), full((1, LANES))],
        out_specs=[row(D_MODEL), pl.BlockSpec((tm * ROW_TILE, LANES), lambda i: (i, 0)), row(LANES), row(LANES),
                   full((1, LANES))],
        out_shape=[jax.ShapeDtypeStruct((n, D_MODEL), F32), jax.ShapeDtypeStruct((n * ROW_TILE, LANES), F32),
                   jax.ShapeDtypeStruct((n, LANES), F32), jax.ShapeDtypeStruct((n, LANES), jnp.int32),
                   jax.ShapeDtypeStruct((1, LANES), jnp.int32)],
        scratch_shapes=[pltpu.VMEM((1, LANES), F32)],
        compiler_params=_cparams(1),
        name="mix",
    )(a_out, *os_, *ls_, z, z, x2d, wr, wa, wo, gf, wrt, brt, cnt0)


DMA_UNROLL = 8


def _dispatch_kernel(dest_ref, xn_ref, *rest, tm):
    xb_ref, sem = rest[-2], rest[-1]

    def copies(r):
        return [_row_tile_copy(xn_ref, r, xb_ref, dest_ref[TOP_K * r + k], sem) for k in range(TOP_K)]

    def issue(r, carry):
        for c in copies(r):
            c.start()
        return carry

    def drain(r, carry):
        for c in copies(r):
            c.wait()
        return carry

    lax.fori_loop(0, tm, issue, 0, unroll=DMA_UNROLL)
    lax.fori_loop(0, tm, drain, 0, unroll=DMA_UNROLL)


def _dispatch(dest_flat, xn, xb, tm, rows):
    n = xn.shape[0] // ROW_TILE
    any_spec = pl.BlockSpec(memory_space=pl.ANY)
    in_specs = [pl.BlockSpec((TOP_K * tm,), lambda i: (i,), memory_space=pltpu.SMEM),
                pl.BlockSpec((tm * ROW_TILE, LANES), lambda i: (i, 0))]
    args = [dest_flat, xn]
    aliases = {}
    if xb is not None:
        in_specs.append(any_spec)
        args.append(xb)
        aliases = {2: 0}
    return pl.pallas_call(
        functools.partial(_dispatch_kernel, tm=tm),
        grid=(n // tm,),
        in_specs=in_specs,
        out_specs=any_spec,
        out_shape=jax.ShapeDtypeStruct((rows * ROW_TILE, LANES), F32),
        scratch_shapes=[pltpu.SemaphoreType.DMA],
        input_output_aliases=aliases,
        compiler_params=_cparams(1),
        name="dispatch",
    )(*args)


def _moe_kernel(be_ref, br_ref, nu_ref, xb_ref, wgu_ref, bgu_ref, wd_ref, bd_ref, y_ref, wgu_bf, wd_bf):
    i = pl.program_id(0)
    changed = jnp.logical_or(i == 0, be_ref[i] != be_ref[jnp.maximum(i - 1, 0)])

    @pl.when(changed)
    def _():
        wgu_bf[...] = wgu_ref[...].astype(BF16)
        wd_bf[...] = wd_ref[...].astype(BF16)

    @pl.when(i < nu_ref[0])
    def _():
        xb = _load_row_tiles(xb_ref, EXPERT_ROWS).astype(BF16)
        gu = jnp.dot(xb, wgu_bf[...], preferred_element_type=F32) + bgu_ref[...]
        g = jnp.minimum(gu[:, :D_EXPERT], SWIGLU_LIMIT)
        u = jnp.clip(gu[:, D_EXPERT:], -SWIGLU_LIMIT, SWIGLU_LIMIT)
        act = (u + 1.0) * (g * jax.nn.sigmoid(SWIGLU_ALPHA * g))
        _store_row_tiles(y_ref, jnp.dot(act.astype(BF16), wd_bf[...], preferred_element_type=F32) + bd_ref[...])


def _moe(block_e, block_row, n_used, xb, wgu, bgu, wd, bd):
    eb = EXPERT_ROWS
    grid_spec = pltpu.PrefetchScalarGridSpec(
        num_scalar_prefetch=3,
        grid=(block_e.shape[0],),
        in_specs=[
            pl.BlockSpec((eb * ROW_TILE, LANES), lambda i, be, br, nu: (br[i], 0)),
            pl.BlockSpec((None, D_MODEL, 2 * D_EXPERT), lambda i, be, br, nu: (be[i], 0, 0)),
            pl.BlockSpec((None, 1, 2 * D_EXPERT), lambda i, be, br, nu: (be[i], 0, 0)),
            pl.BlockSpec((None, D_EXPERT, D_MODEL), lambda i, be, br, nu: (be[i], 0, 0)),
            pl.BlockSpec((None, 1, D_MODEL), lambda i, be, br, nu: (be[i], 0, 0)),
        ],
        out_specs=pl.BlockSpec((eb * ROW_TILE, LANES), lambda i, be, br, nu: (br[i], 0)),
        scratch_shapes=[pltpu.VMEM((D_MODEL, 2 * D_EXPERT), BF16), pltpu.VMEM((D_EXPERT, D_MODEL), BF16)],
    )
    return pl.pallas_call(
        _moe_kernel,
        grid_spec=grid_spec,
        out_shape=jax.ShapeDtypeStruct(xb.shape, F32),
        compiler_params=_cparams(1),
        name="moe",
    )(block_e, block_row, n_used, xb, wgu, bgu, wd, bd)


def _expert_blocks(counts, cap, n_assign):
    eb = EXPERT_ROWS
    n_grid = -(-n_assign // eb) + N_EXPERTS
    nblk = (counts + eb - 1) // eb
    blk_end = jnp.cumsum(nblk)
    n_used = blk_end[-1:].astype(jnp.int32)
    i = jnp.minimum(jnp.arange(n_grid, dtype=jnp.int32), n_used[0] - 1)
    block_e = jnp.sum(blk_end[None, :] <= i[:, None], axis=1).astype(jnp.int32)
    block_row = block_e * (cap // eb) + i - (blk_end - nblk)[block_e]
    return block_e, block_row.astype(jnp.int32), n_used


def _ple_kernel(dest_ref, x1_ref, gate_ref, p_ref, gp_ref, wg_ref, wp_ref, yb_ref, y_ref, ybuf, sem):
    tm = x1_ref.shape[0]

    def copies(r):
        return [_row_tile_copy(yb_ref, dest_ref[TOP_K * r + k], ybuf.at[k], r, sem) for k in range(TOP_K)]

    def issue(r, carry):
        for c in copies(r):
            c.start()
        return carry

    def drain(r, carry):
        for c in copies(r):
            c.wait()
        return carry

    lax.fori_loop(0, tm, issue, 0, unroll=DMA_UNROLL)
    lax.fori_loop(0, tm, drain, 0, unroll=DMA_UNROLL)

    gates = gate_ref[...]
    x2 = x1_ref[...]
    for k in range(TOP_K):
        x2 = x2 + _load_row_tiles(ybuf.at[k], tm) * gates[:, k:k + 1]
    gate = jax.nn.sigmoid(jnp.dot(_rms(x2, gp_ref[...]).astype(BF16), wg_ref[...], preferred_element_type=F32))
    proj = jnp.dot(p_ref[...].astype(BF16), wp_ref[...], preferred_element_type=F32)
    y_ref[...] = x2 + gate * proj


def _ple(dest_flat, x1, gates, p2d, gp, wg, wp, yb, tm):
    n = x1.shape[0]
    row = lambda w: pl.BlockSpec((tm, w), lambda i: (i, 0))
    full = lambda shape: pl.BlockSpec(shape, lambda i: (0,) * len(shape))
    return pl.pallas_call(
        _ple_kernel,
        grid=(n // tm,),
        in_specs=[pl.BlockSpec((TOP_K * tm,), lambda i: (i,), memory_space=pltpu.SMEM),
                  row(D_MODEL), row(LANES), row(D_PLE), full((1, D_MODEL)),
                  full((D_MODEL, D_MODEL)), full((D_PLE, D_MODEL)), pl.BlockSpec(memory_space=pl.ANY)],
        out_specs=row(D_MODEL),
        out_shape=jax.ShapeDtypeStruct((n, D_MODEL), F32),
        scratch_shapes=[pltpu.VMEM((TOP_K, tm * ROW_TILE, LANES), F32), pltpu.SemaphoreType.DMA],
        compiler_params=_cparams(1),
        name="ple",
    )(dest_flat, x1, gates, p2d, gp, wg, wp, yb)


def kernel(x_prompt, x_sample, state_conv, state_h, cache_kv_w128, cache_kv_w512, cache_kv_w2048, p_prompt,
           p_sample, norm_mix_g, w_in, conv_w, conv_b, w_rg_a, b_rg_a, w_rg_x, b_rg_x, lru_lambda, q_norm_g,
           k_norm_g, rel_bias, w_br_rnn, w_br_attn, w_out, norm_ffn_g, w_router, b_router, w_gate_up, b_gate_up,
           w_down, b_down, norm_ple_g, w_ple_gate, w_ple_proj):
    bp, seq, _ = x_prompt.shape
    bs = x_sample.shape[0]
    n_p = bp * seq
    li = 0
    row = lambda v: v.reshape(1, -1).astype(F32)

    w = w_in[li]
    o_q, o_ga = 2 * D_RNN, 2 * D_RNN + 3 * D_QKV
    w_perm = jnp.concatenate([w[:, :o_q], w[:, o_ga:], w[:, o_q:o_ga]], axis=1).astype(BF16)
    colgain = jnp.concatenate([jnp.ones((COL_Q,), F32), jnp.tile(q_norm_g[li], N_ATTN_HEADS),
                               jnp.tile(k_norm_g[li], N_ATTN_HEADS), jnp.ones((D_QKV,), F32)]).reshape(1, D_IN)
    wax = jnp.concatenate([w_rg_a[li], w_rg_x[li]], axis=-1).astype(BF16)
    lru = (conv_w[li].astype(F32), row(conv_b[li]), wax, row(b_rg_a[li]), row(b_rg_x[li]), row(lru_lambda[li]))
    wr, wa, wo = w_br_rnn[li].astype(BF16), w_br_attn[li].astype(BF16), w_out[li].astype(BF16)
    wrt = jnp.pad(w_router[li].astype(BF16), ((0, 0), (0, LANES - N_EXPERTS)))
    brt = jnp.pad(row(b_router[li]), ((0, 0), (0, LANES - N_EXPERTS)))
    gf = row(norm_ffn_g[li])
    biases = [_group_bias(rel_bias, gi, wnd, dil) for gi, (wnd, dil) in enumerate(GROUPS)]

    xp = x_prompt.reshape(n_p, D_MODEL)
    zp = _inproj(xp, row(norm_mix_g[li]), w_perm, colgain, tm=2048, out_dtype=BF16)
    a_p, h_p = _rglru_prompt(zp, bp, seq, 512, *lru)
    os_p, ls_p = [], []
    for gi in range(len(GROUPS)):
        o, l = _attn_prompt(zp, _band_bias(biases[gi]), gi, bp, seq)
        os_p.append(o)
        ls_p.append(l)
    cap = -(-(n_p + bs) // EXPERT_ROWS) * EXPERT_ROWS
    x1_p, xn_p, gate_p, dest_p, cnt_p = _mix(a_p, os_p, ls_p, zp, xp, wr, wa, wo, gf, wrt, brt,
                                              jnp.zeros((1, LANES), jnp.int32), tm=256, cap=cap)

    xs = x_sample.reshape(bs, D_MODEL)
    zs = _inproj(xs, row(norm_mix_g[li]), w_perm, colgain, tm=bs, out_dtype=F32)
    sc = state_conv[li]
    a_s, h_s = _rglru_step(zs, [sc[:, j] for j in range(CONV_WIDTH - 1)], state_h[li], *lru)
    heads = lambda c0: zs[:, c0:c0 + D_QKV].astype(F32).reshape(bs, N_ATTN_HEADS, HEAD_DIM)
    caches = [c[li].reshape(bs, QUERY_BLOCK, dil, 2, HEADS_PER_GROUP, HEAD_DIM)
              for c, (_, dil) in zip((cache_kv_w128, cache_kv_w512, cache_kv_w2048), GROUPS)]
    bias_c = jnp.stack([b[:, :0:-1].T for b in biases])[..., None]
    bias_n = jnp.stack([b[:, 0] for b in biases])[..., None]
    o_s, l_s = _attn_step(heads(COL_Q), heads(COL_K), heads(COL_V), caches, bias_c, bias_n, bb=4)
    os_s = [o_s[g].reshape(bs, D_ATTN_OUT) for g in range(len(GROUPS))]
    ls_s = [l_s[g].reshape(bs, D_ATTN_OUT) for g in range(len(GROUPS))]
    x1_s, xn_s, gate_s, dest_s, cnt_s = _mix(a_s, os_s, ls_s, zs, xs, wr, wa, wo, gf, wrt, brt, cnt_p, tm=bs, cap=cap)

    dflat_p = dest_p[:, :TOP_K].reshape(-1)
    dflat_s = dest_s[:, :TOP_K].reshape(-1)
    xb = _dispatch(dflat_p, xn_p, None, 256, N_EXPERTS * cap)
    xb = _dispatch(dflat_s, xn_s, xb, bs, N_EXPERTS * cap)
    block_e, block_row, n_used = _expert_blocks(cnt_s[0, :N_EXPERTS], cap, (n_p + bs) * TOP_K)
    yb = _moe(block_e, block_row, n_used, xb, w_gate_up[li], b_gate_up[li][:, None, :],
              w_down[li], b_down[li][:, None, :])

    gp, wg, wp = row(norm_ple_g[li]), w_ple_gate[li].astype(BF16), w_ple_proj[li].astype(BF16)
    y_p = _ple(dflat_p, x1_p, gate_p, p_prompt[li].reshape(n_p, D_PLE), gp, wg, wp, yb, tm=256)
    y_s = _ple(dflat_s, x1_s, gate_s, p_sample[li].reshape(bs, D_PLE), gp, wg, wp, yb, tm=bs)

    z3 = zp.reshape(bp, seq, D_IN)
    zs3 = zs.reshape(bs, 1, D_IN)

    def new_kv(zz, gi, keep):
        t = zz.shape[1]
        c = gi * D_ATTN_OUT
        kk = zz[:, t - keep:, COL_K + c:COL_K + c + D_ATTN_OUT].astype(F32)
        vv = zz[:, t - keep:, COL_V + c:COL_V + c + D_ATTN_OUT].astype(F32)
        shape = (zz.shape[0], keep, HEADS_PER_GROUP, HEAD_DIM)
        return jnp.stack([kk.reshape(shape), vv.reshape(shape)], axis=2)[None]

    conv_p = z3[:, seq - (CONV_WIDTH - 1):, COL_XR:COL_XR + D_RNN].astype(F32)[None]
    conv_s = jnp.concatenate([sc[:, 1:], zs3[:, :, COL_XR:COL_XR + D_RNN].astype(F32)], axis=1)[None]
    outs = [y_p.reshape(bp, seq, D_MODEL), y_s.reshape(bs, 1, D_MODEL), conv_p, conv_s,
            h_p.reshape(1, bp, D_RNN), h_s.reshape(1, bs, D_RNN)]
    for gi, (wnd, _) in enumerate(GROUPS):
        outs.append(new_kv(z3, gi, min(wnd, seq)))
        outs.append(new_kv(zs3, gi, 1))
    return tuple(outs)
```

```python
import functools
import math

import numpy as np
import jax
import jax.numpy as jnp
from jax import lax
from jax.experimental import pallas as pl
from jax.experimental.pallas import tpu as pltpu

F32 = jnp.float32
BF16 = jnp.bfloat16

D_MODEL = 1024
D_RNN = 1024
N_RNN_BLOCKS = 8
RNN_BLOCK = D_RNN // N_RNN_BLOCKS
CONV_WIDTH = 4
LRU_C = 8.0
GROUPS = ((128, 1), (512, 4), (2048, 16))
HEADS_PER_GROUP = 4
HEAD_DIM = 128
N_ATTN_HEADS = HEADS_PER_GROUP * len(GROUPS)
D_QKV = N_ATTN_HEADS * HEAD_DIM
D_ATTN_OUT = HEADS_PER_GROUP * HEAD_DIM
N_BUCKETS = 32
MAX_DISTANCE = 2048
QUERY_BLOCK = 128
D_IN = 2 * D_RNN + 3 * D_QKV + 2 * D_MODEL
N_EXPERTS = 32
TOP_K = 4
D_EXPERT = 1024
SWIGLU_ALPHA = 1.702
SWIGLU_LIMIT = 7.0
D_PLE = 256
EPS = 1e-6
ATTN_SCALE = HEAD_DIM ** -0.5
MASK_VALUE = -1e30

LANES = 128
ROW_TILE = D_MODEL // LANES
COL_XR, COL_YG, COL_GA, COL_GB = 0, D_RNN, 2 * D_RNN, 2 * D_RNN + D_MODEL
COL_Q = 2 * D_RNN + 2 * D_MODEL
COL_K = COL_Q + D_QKV
COL_V = COL_K + D_QKV
INPROJ_TN = 512
INPROJ_SUB = 512
ATTN_SUPER = 2048
EXPERT_ROWS = 256
VMEM_LIMIT = 56 * 1024 * 1024


def _cparams(n_axes):
    return pltpu.CompilerParams(dimension_semantics=("arbitrary",) * n_axes, vmem_limit_bytes=VMEM_LIMIT)


def _rms(x, g):
    ms = jnp.mean(x * x, axis=-1, keepdims=True)
    return (x * lax.rsqrt(ms + EPS)) * g


def _round_bf16(x):
    return x.astype(BF16).astype(F32)


def _load_row_tiles(ref, n_rows):
    return jnp.concatenate([ref[pl.ds(c, n_rows, stride=ROW_TILE), :] for c in range(ROW_TILE)], axis=1)


def _store_row_tiles(ref, value):
    n_rows = value.shape[0]
    for c in range(ROW_TILE):
        ref[pl.ds(c, n_rows, stride=ROW_TILE), :] = value[:, c * LANES:(c + 1) * LANES]


def _row_tile_copy(src_ref, src_row, dst_ref, dst_row, sem):
    src = src_ref.at[pl.ds(pl.multiple_of(src_row * ROW_TILE, ROW_TILE), ROW_TILE)]
    dst = dst_ref.at[pl.ds(pl.multiple_of(dst_row * ROW_TILE, ROW_TILE), ROW_TILE)]
    return pltpu.make_async_copy(src, dst, sem)


def _inproj_kernel(x_ref, g_ref, w_ref, cg_ref, z_ref, u_scr, *, sub):
    j = pl.program_id(1)
    n_sub = x_ref.shape[0] // sub

    @pl.when(j == 0)
    def _():
        for s in range(n_sub):
            rows = slice(s * sub, (s + 1) * sub)
            u_scr[rows, :] = _rms(x_ref[rows, :], g_ref[...]).astype(BF16)

    is_qk = jnp.logical_and(j >= COL_Q // INPROJ_TN, j < COL_V // INPROJ_TN)
    for s in range(n_sub):
        rows = slice(s * sub, (s + 1) * sub)
        acc = jnp.dot(u_scr[rows, :], w_ref[...], preferred_element_type=F32)
        for h in range(INPROJ_TN // HEAD_DIM):
            cs = slice(h * HEAD_DIM, (h + 1) * HEAD_DIM)
            a = acc[:, cs]
            z_ref[rows, cs] = jnp.where(is_qk, _rms(a, cg_ref[:, cs]), a).astype(z_ref.dtype)


def _inproj(x2d, g, w_perm, colgain, tm, out_dtype):
    n = x2d.shape[0]
    return pl.pallas_call(
        functools.partial(_inproj_kernel, sub=min(tm, INPROJ_SUB)),
        grid=(n // tm, D_IN // INPROJ_TN),
        in_specs=[
            pl.BlockSpec((tm, D_MODEL), lambda i, j: (i, 0)),
            pl.BlockSpec((1, D_MODEL), lambda i, j: (0, 0)),
            pl.BlockSpec((D_MODEL, INPROJ_TN), lambda i, j: (0, j)),
            pl.BlockSpec((1, INPROJ_TN), lambda i, j: (0, j)),
        ],
        out_specs=pl.BlockSpec((tm, INPROJ_TN), lambda i, j: (i, j)),
        out_shape=jax.ShapeDtypeStruct((n, D_IN), out_dtype),
        scratch_shapes=[pltpu.VMEM((tm, D_MODEL), BF16)],
        compiler_params=_cparams(2),
        name="inproj",
    )(x2d, g, w_perm, colgain)


GELU_C1 = 2.0 * math.sqrt(2.0 / math.pi)
GELU_C2 = GELU_C1 * 0.044715


def _gelu_tanh(x):
    return x / (1.0 + jnp.exp(x * (-GELU_C1 - GELU_C2 * (x * x))))


def _softplus(v):
    return jnp.maximum(v, 0.0) + jnp.log1p(jnp.exp(-jnp.abs(v)))


def _lru_gates_block(xc, n, wax_ref, ba_ref, bx_ref, lam_ref):
    cs = slice(n * RNN_BLOCK, (n + 1) * RNN_BLOCK)
    ri = jnp.dot(xc.astype(BF16), wax_ref[n], preferred_element_type=F32)
    r = jax.nn.sigmoid(ri[:, :RNN_BLOCK] + ba_ref[:, cs])
    i = jax.nn.sigmoid(ri[:, RNN_BLOCK:] + bx_ref[:, cs])
    log_a = (-LRU_C * r) * _softplus(-lam_ref[:, cs])
    a = jnp.exp(log_a)
    v = 1.0 - a * a
    b = jnp.where(v > 0.0, v * lax.rsqrt(v), 0.0) * (i * xc)
    return a, b


def _rglru_kernel(xr_ref, yg_ref, cw_ref, cb_ref, wax_ref, ba_ref, bx_ref, lam_ref,
                  aout_ref, hlast_ref, xe_scr, a_scr, b_scr, hcar_scr, *, tc):
    c = pl.program_id(1)
    sub = 8

    @pl.when(c == 0)
    def _():
        xe_scr[0:8, :] = jnp.zeros((8, D_RNN), F32)
        hcar_scr[...] = jnp.zeros((sub, D_RNN), F32)

    xe_scr[8:8 + tc, :] = xr_ref[...].astype(F32)
    for n in range(N_RNN_BLOCKS):
        cs = slice(n * RNN_BLOCK, (n + 1) * RNN_BLOCK)
        xc = cb_ref[:, cs]
        for j in range(CONV_WIDTH):
            xc = xc + xe_scr[pl.ds(8 - (CONV_WIDTH - 1) + j, tc), cs] * cw_ref[j:j + 1, cs]
        a, b = _lru_gates_block(xc, n, wax_ref, ba_ref, bx_ref, lam_ref)
        a_scr[n] = a
        b_scr[n] = b
    xe_scr[0:8, :] = xe_scr[tc:tc + 8, :]

    row = lax.broadcasted_iota(jnp.int32, (sub, RNN_BLOCK), 0)

    def step(g, carry):
        rows = pl.ds(pl.multiple_of(g * sub, sub), sub)
        new = []
        for n in range(N_RNN_BLOCKS):
            a = a_scr[n, rows, :]
            b = b_scr[n, rows, :]
            for d in (1, 2, 4):
                keep = row >= d
                b = a * jnp.where(keep, pltpu.roll(b, d, 0), 0.0) + b
                a = a * jnp.where(keep, pltpu.roll(a, d, 0), 1.0)
            h = b + a * carry[n]
            b_scr[n, rows, :] = h
            new.append(jnp.broadcast_to(h[sub - 1:sub, :], (sub, RNN_BLOCK)))
        return tuple(new)

    init = tuple(hcar_scr[:, n * RNN_BLOCK:(n + 1) * RNN_BLOCK] for n in range(N_RNN_BLOCKS))
    h_in = lax.fori_loop(0, tc // sub, step, init)

    for n in range(N_RNN_BLOCKS):
        cs = slice(n * RNN_BLOCK, (n + 1) * RNN_BLOCK)
        aout_ref[:, cs] = (b_scr[n] * _gelu_tanh(yg_ref[:, cs].astype(F32))).astype(BF16)
        hcar_scr[:, cs] = h_in[n]
        hlast_ref[:, cs] = h_in[n][0:1, :]


def _rglru_prompt(z, batch, seq, tc, cw, cb, wax, ba, bx, lam):
    nc = seq // tc
    vec = lambda r: pl.BlockSpec((r, D_RNN), lambda b, c: (0, 0))
    return pl.pallas_call(
        functools.partial(_rglru_kernel, tc=tc),
        grid=(batch, nc),
        in_specs=[
            pl.BlockSpec((tc, D_RNN), lambda b, c: (b * nc + c, COL_XR // D_RNN)),
            pl.BlockSpec((tc, D_RNN), lambda b, c: (b * nc + c, COL_YG // D_RNN)),
            vec(CONV_WIDTH), vec(1),
            pl.BlockSpec((N_RNN_BLOCKS, RNN_BLOCK, 2 * RNN_BLOCK), lambda b, c: (0, 0, 0)),
            vec(1), vec(1), vec(1),
        ],
        out_specs=[
            pl.BlockSpec((tc, D_RNN), lambda b, c: (b * nc + c, 0)),
            pl.BlockSpec((None, 1, D_RNN), lambda b, c: (b, 0, 0)),
        ],
        out_shape=[jax.ShapeDtypeStruct((batch * seq, D_RNN), BF16),
                   jax.ShapeDtypeStruct((batch, 1, D_RNN), F32)],
        scratch_shapes=[pltpu.VMEM((tc + 8, D_RNN), F32)]
        + [pltpu.VMEM((N_RNN_BLOCKS, tc, RNN_BLOCK), F32)] * 2
        + [pltpu.VMEM((8, D_RNN), F32)],
        compiler_params=_cparams(2),
        name="rglru",
    )(z, z, cw, cb, wax, ba, bx, lam)


def _rglru_step_kernel(xr_ref, yg_ref, c0_ref, c1_ref, c2_ref, h0_ref, cw_ref, cb_ref, wax_ref, ba_ref,
                       bx_ref, lam_ref, aout_ref, hout_ref):
    taps = (c0_ref, c1_ref, c2_ref)
    for n in range(N_RNN_BLOCKS):
        cs = slice(n * RNN_BLOCK, (n + 1) * RNN_BLOCK)
        xc = cb_ref[:, cs]
        for j in range(CONV_WIDTH - 1):
            xc = xc + taps[j][:, cs] * cw_ref[j:j + 1, cs]
        xc = xc + xr_ref[:, cs].astype(F32) * cw_ref[CONV_WIDTH - 1:CONV_WIDTH, cs]
        a, b = _lru_gates_block(xc, n, wax_ref, ba_ref, bx_ref, lam_ref)
        h = a * h0_ref[:, cs] + b
        hout_ref[:, cs] = h
        aout_ref[:, cs] = (h * _gelu_tanh(yg_ref[:, cs].astype(F32))).astype(BF16)


def _rglru_step(z, conv_taps, h0, cw, cb, wax, ba, bx, lam):
    n = z.shape[0]
    full = lambda shape: pl.BlockSpec(shape, lambda i: (0,) * len(shape))
    return pl.pallas_call(
        _rglru_step_kernel,
        grid=(1,),
        in_specs=[
            pl.BlockSpec((n, D_RNN), lambda i: (0, COL_XR // D_RNN)),
            pl.BlockSpec((n, D_RNN), lambda i: (0, COL_YG // D_RNN)),
            full((n, D_RNN)), full((n, D_RNN)), full((n, D_RNN)), full((n, D_RNN)),
            full((CONV_WIDTH, D_RNN)), full((1, D_RNN)),
            full((N_RNN_BLOCKS, RNN_BLOCK, 2 * RNN_BLOCK)),
            full((1, D_RNN)), full((1, D_RNN)), full((1, D_RNN)),
        ],
        out_specs=[full((n, D_RNN)), full((n, D_RNN))],
        out_shape=[jax.ShapeDtypeStruct((n, D_RNN), BF16), jax.ShapeDtypeStruct((n, D_RNN), F32)],
        compiler_params=_cparams(1),
        name="rglru_step",
    )(z, z, *conv_taps, h0, cw, cb, wax, ba, bx, lam)


def _t5_causal_buckets(dist):
    max_exact = N_BUCKETS // 2
    d = np.maximum(dist, 1).astype(np.float32)
    large = max_exact + (np.log(d / max_exact) / np.log(MAX_DISTANCE / max_exact)
                         * (N_BUCKETS - max_exact)).astype(np.int32)
    large = np.minimum(large, N_BUCKETS - 1)
    return np.where(dist < max_exact, dist, large).astype(np.int32)


def _group_bias(rel_bias, gi, window, dilation):
    buckets = _t5_causal_buckets(np.arange(window // dilation + 1) * dilation)
    heads = slice(gi * HEADS_PER_GROUP, (gi + 1) * HEADS_PER_GROUP)
    return rel_bias[buckets][:, heads].T.astype(F32)


def _band_bias(bias):
    nh, nk = bias.shape
    qb, span = QUERY_BLOCK, 2 * QUERY_BLOCK
    assert nk == qb + 1
    w = jnp.concatenate([jnp.full((nh, qb - 1), MASK_VALUE, F32), bias[:, ::-1], jnp.full((nh, qb), MASK_VALUE, F32)],
                        axis=1)
    skew = jnp.tile(w, (1, qb + 1))[:, :qb * (w.shape[1] + 1)].reshape(nh, qb, w.shape[1] + 1)
    return skew[:, ::-1, :span]


def _attn_kernel(q_ref, k_ref, v_ref, bias_ref, o_ref, lse_ref, q_scr, kc_scr, vc_scr, o_scr, l_scr, *, d, sup):
    sb = pl.program_id(2)
    win = QUERY_BLOCK * d

    @pl.when(sb == 0)
    def _():
        kc_scr[0:win, :] = jnp.zeros((win, HEAD_DIM), F32)
        vc_scr[0:win, :] = jnp.zeros((win, HEAD_DIM), F32)

    q_scr[...] = q_ref[...].astype(F32)
    kc_scr[win:win + sup, :] = k_ref[...].astype(F32)
    vc_scr[win:win + sup, :] = v_ref[...].astype(F32)
    bias = bias_ref[...]
    col = lax.broadcasted_iota(jnp.int32, (QUERY_BLOCK, 2 * QUERY_BLOCK), 1)

    def rows(start, size):
        return pl.ds(start, size) if d == 1 else pl.ds(start, size, stride=d)

    for r in range(d):
        for qb in range(sup // win):
            start = r + win * qb
            qu = q_scr[rows(start, QUERY_BLOCK), :].astype(BF16)
            ku = kc_scr[rows(start, 2 * QUERY_BLOCK), :].astype(BF16)
            vu = vc_scr[rows(start, 2 * QUERY_BLOCK), :].astype(BF16)
            s = lax.dot_general(qu, ku, (((1,), (1,)), ((), ())), preferred_element_type=F32) * ATTN_SCALE + bias
            if qb == 0:
                s = jnp.where(jnp.logical_or(sb > 0, col >= QUERY_BLOCK), s, MASK_VALUE)
            m = jnp.max(s, axis=-1, keepdims=True)
            e = jnp.exp(s - m)
            den = jnp.sum(e, axis=-1, keepdims=True)
            o_scr[rows(start, QUERY_BLOCK), :] = jnp.dot((e / den).astype(BF16), vu, preferred_element_type=F32)
            l_scr[rows(start, QUERY_BLOCK), :] = jnp.broadcast_to(m + jnp.log(den), (QUERY_BLOCK, HEAD_DIM))

    o_ref[...] = o_scr[...].astype(BF16)
    lse_ref[...] = l_scr[...]
    kc_scr[0:win, :] = kc_scr[sup:sup + win, :]
    vc_scr[0:win, :] = vc_scr[sup:sup + win, :]


def _attn_prompt(z, band, gi, batch, seq):
    d = GROUPS[gi][1]
    sup = ATTN_SUPER
    win = QUERY_BLOCK * d
    nsb = seq // sup
    qb0 = COL_Q // HEAD_DIM + gi * HEADS_PER_GROUP
    kb0 = COL_K // HEAD_DIM + gi * HEADS_PER_GROUP
    vb0 = COL_V // HEAD_DIM + gi * HEADS_PER_GROUP
    blk = lambda c0: pl.BlockSpec((sup, HEAD_DIM), lambda b, h, s: (b * nsb + s, c0 + h))
    out = pl.BlockSpec((sup, HEAD_DIM), lambda b, h, s: (b * nsb + s, h))
    return pl.pallas_call(
        functools.partial(_attn_kernel, d=d, sup=sup),
        grid=(batch, HEADS_PER_GROUP, nsb),
        in_specs=[blk(qb0), blk(kb0), blk(vb0),
                  pl.BlockSpec((None, QUERY_BLOCK, 2 * QUERY_BLOCK), lambda b, h, s: (h, 0, 0))],
        out_specs=[out, out],
        out_shape=[jax.ShapeDtypeStruct((batch * seq, D_ATTN_OUT), BF16),
                   jax.ShapeDtypeStruct((batch * seq, D_ATTN_OUT), F32)],
        scratch_shapes=[pltpu.VMEM((sup, HEAD_DIM), F32), pltpu.VMEM((win + sup, HEAD_DIM), F32),
                        pltpu.VMEM((win + sup, HEAD_DIM), F32), pltpu.VMEM((sup, HEAD_DIM), F32),
                        pltpu.VMEM((sup, HEAD_DIM), F32)],
        compiler_params=_cparams(3),
        name=f"attn_g{gi}",
    )(z, z, z, band)


def _attn_step_kernel(q_ref, kn_ref, vn_ref, c0_ref, c1_ref, c2_ref, bc_ref, bn_ref, o_ref, lse_ref, *, bb):
    caches = (c0_ref, c1_ref, c2_ref)
    for b in range(bb):
        for g in range(len(GROUPS)):
            hs = slice(g * HEADS_PER_GROUP, (g + 1) * HEADS_PER_GROUP)
            qg = _round_bf16(q_ref[b, hs, :])
            kc = _round_bf16(caches[g][b, :, 0])
            vc = _round_bf16(caches[g][b, :, 1])
            kn = _round_bf16(kn_ref[b, hs, :])
            vn = _round_bf16(vn_ref[b, hs, :])
            sc = jnp.sum(kc * qg[None], axis=-1, keepdims=True) * ATTN_SCALE + bc_ref[g]
            sn = jnp.sum(kn * qg, axis=-1, keepdims=True) * ATTN_SCALE + bn_ref[g]
            m = jnp.maximum(jnp.max(sc, axis=0), sn)
            ec = jnp.exp(sc - m[None])
            en = jnp.exp(sn - m)
            den = jnp.sum(ec, axis=0) + en
            o = jnp.sum(_round_bf16(ec / den[None]) * vc, axis=0) + _round_bf16(en / den) * vn
            o_ref[g, b] = o
            lse_ref[g, b] = jnp.broadcast_to(m + jnp.log(den), (HEADS_PER_GROUP, HEAD_DIM))


def _attn_step(q, kn, vn, caches, bias_c, bias_n, bb):
    n = q.shape[0]
    ng = len(GROUPS)
    tok = pl.BlockSpec((bb, N_ATTN_HEADS, HEAD_DIM), lambda i: (i, 0, 0))
    cache = pl.BlockSpec((bb, QUERY_BLOCK, None, 2, HEADS_PER_GROUP, HEAD_DIM), lambda i: (i, 0, 0, 0, 0, 0))
    out = pl.BlockSpec((ng, bb, HEADS_PER_GROUP, HEAD_DIM), lambda i: (0, i, 0, 0))
    return pl.pallas_call(
        functools.partial(_attn_step_kernel, bb=bb),
        grid=(n // bb,),
        in_specs=[tok, tok, tok, cache, cache, cache,
                  pl.BlockSpec((ng, QUERY_BLOCK, HEADS_PER_GROUP, 1), lambda i: (0, 0, 0, 0)),
                  pl.BlockSpec((ng, HEADS_PER_GROUP, 1), lambda i: (0, 0, 0))],
        out_specs=[out, out],
        out_shape=[jax.ShapeDtypeStruct((ng, n, HEADS_PER_GROUP, HEAD_DIM), F32)] * 2,
        compiler_params=_cparams(1),
        name="attn_step",
    )(q, kn, vn, *caches, bias_c, bias_n)


def _mix_kernel(a_ref, o0_ref, o1_ref, o2_ref, l0_ref, l1_ref, l2_ref, ga_ref, gb_ref, x_ref,
                wr_ref, wa_ref, wo_ref, gf_ref, wrt_ref, brt_ref, cnt0_ref,
                x1_ref, xn_ref, gate_ref, dest_ref, cnt_ref, cnt_scr, *, cap):
    tm = x_ref.shape[0]

    @pl.when(pl.program_id(0) == 0)
    def _():
        cnt_scr[...] = cnt0_ref[...].astype(F32)

    l0, l1, l2 = l0_ref[...], l1_ref[...], l2_ref[...]
    mx = jnp.maximum(jnp.maximum(l0, l1), l2)
    e0, e1, e2 = jnp.exp(l0 - mx), jnp.exp(l1 - mx), jnp.exp(l2 - mx)
    den = e0 + e1 + e2
    b_out = (_round_bf16(e0 / den) * _round_bf16(o0_ref[...]) + _round_bf16(e1 / den) * _round_bf16(o1_ref[...])
             + _round_bf16(e2 / den) * _round_bf16(o2_ref[...]))
    rnn = jnp.dot(a_ref[...], wr_ref[...], preferred_element_type=F32)
    att = jnp.dot(b_out.astype(BF16), wa_ref[...], preferred_element_type=F32)
    merged = jax.nn.sigmoid(ga_ref[...].astype(F32)) * rnn + jax.nn.sigmoid(gb_ref[...].astype(F32)) * att
    x1 = x_ref[...] + jnp.dot(merged.astype(BF16), wo_ref[...], preferred_element_type=F32)
    x1_ref[...] = x1
    xn = _rms(x1, gf_ref[...])
    _store_row_tiles(xn_ref, xn)

    logits = jnp.dot(xn.astype(BF16), wrt_ref[...], preferred_element_type=F32) + brt_ref[...]
    lane = lax.broadcasted_iota(jnp.int32, (tm, LANES), 1)
    cur = jnp.where(lane < N_EXPERTS, logits, -jnp.inf)
    vals, idxs = [], []
    for _ in range(TOP_K):
        mk = jnp.max(cur, axis=-1, keepdims=True)
        ik = jnp.min(jnp.where(cur == mk, lane, LANES), axis=-1, keepdims=True)
        vals.append(mk)
        idxs.append(ik)
        cur = jnp.where(lane == ik, -jnp.inf, cur)
    es = [jnp.exp(v - vals[0]) for v in vals]
    den = es[0] + es[1] + es[2] + es[3]
    chosen = jnp.zeros((tm, LANES), F32)
    for k in range(TOP_K):
        chosen = chosen + jnp.where(lane == idxs[k], 1.0, 0.0)
    tri = jnp.where(lax.broadcasted_iota(jnp.int32, (tm, tm), 1) <= lax.broadcasted_iota(jnp.int32, (tm, tm), 0),
                    1.0, 0.0).astype(BF16)
    incl = jnp.dot(tri, chosen.astype(BF16), preferred_element_type=F32)
    rank = cnt_scr[...] + incl - chosen
    gate_out = jnp.zeros((tm, LANES), F32)
    dest_out = jnp.zeros((tm, LANES), jnp.int32)
    for k in range(TOP_K):
        rank_k = jnp.sum(jnp.where(lane == idxs[k], rank, 0.0), axis=-1, keepdims=True).astype(jnp.int32)
        gate_out = jnp.where(lane == k, es[k] / den, gate_out)
        dest_out = jnp.where(lane == k, idxs[k] * cap + rank_k, dest_out)
    gate_ref[...] = gate_out
    dest_ref[...] = dest_out
    cnt_scr[...] = cnt_scr[...] + incl[tm - 1:tm, :]
    cnt_ref[...] = cnt_scr[...].astype(jnp.int32)


def _mix(a_out, os_, ls_, z, x2d, wr, wa, wo, gf, wrt, brt, cnt0, tm, cap):
    n = x2d.shape[0]
    row = lambda w, c=0: pl.BlockSpec((tm, w), lambda i: (i, c))
    full = lambda shape: pl.BlockSpec(shape, lambda i: (0,) * len(shape))
    return pl.pallas_call(
        functools.partial(_mix_kernel, cap=cap),
        grid=(n // tm,),
        in_specs=[row(D_RNN), row(D_ATTN_OUT), row(D_ATTN_OUT), row(D_ATTN_OUT),
                  row(D_ATTN_OUT), row(D_ATTN_OUT), row(D_ATTN_OUT),
                  row(D_MODEL, COL_GA // D_MODEL), row(D_MODEL, COL_GB // D_MODEL), row(D_MODEL),
                  full((D_RNN, D_MODEL)), full((D_ATTN_OUT, D_MODEL)), full((D_MODEL, D_MODEL)),
                  full((1, D_MODEL)), full((D_MODEL, LANES)), full((1, LANES)), full((1, LANES))],
        out_specs=[row(D_MODEL), pl.BlockSpec((tm * ROW_TILE, LANES), lambda i: (i, 0)), row(LANES), row(LANES),
                   full((1, LANES))],
        out_shape=[jax.ShapeDtypeStruct((n, D_MODEL), F32), jax.ShapeDtypeStruct((n * ROW_TILE, LANES), F32),
                   jax.ShapeDtypeStruct((n, LANES), F32), jax.ShapeDtypeStruct((n, LANES), jnp.int32),
                   jax.ShapeDtypeStruct((1, LANES), jnp.int32)],
        scratch_shapes=[pltpu.VMEM((1, LANES), F32)],
        compiler_params=_cparams(1),
        name="mix",
    )(a_out, *os_, *ls_, z, z, x2d, wr, wa, wo, gf, wrt, brt, cnt0)


DMA_UNROLL = 8


def _start_rows(tm, copies):
    def body(r, carry):
        for c in copies(r):
            c.start()
        return carry

    lax.fori_loop(0, tm, body, 0, unroll=DMA_UNROLL)


def _wait_rows(tm, copies):
    def body(r, carry):
        for c in copies(r):
            c.wait()
        return carry

    lax.fori_loop(0, tm, body, 0, unroll=DMA_UNROLL)


def _dispatch_kernel(dprev_ref, dcur_ref, xn_hbm, *rest, tm):
    xb_ref, src, load_sem, row_sem = rest[-4:]
    i = pl.program_id(0)
    n = pl.num_programs(0)
    slot = i % 2

    def load(tile, s):
        rows = pl.ds(pl.multiple_of(tile * (tm * ROW_TILE), ROW_TILE), tm * ROW_TILE)
        return pltpu.make_async_copy(xn_hbm.at[rows], src.at[s], load_sem.at[s])

    def scatter(dest_ref, s):
        return lambda r: [_row_tile_copy(src.at[s], r, xb_ref, dest_ref[TOP_K * r + k], row_sem.at[s])
                          for k in range(TOP_K)]

    @pl.when(i == 0)
    def _():
        load(0, 0).start()

    @pl.when(i > 0)
    def _():
        _wait_rows(tm, scatter(dprev_ref, 1 - slot))

    @pl.when(i + 1 < n)
    def _():
        load(i + 1, 1 - slot).start()

    load(i, slot).wait()
    _start_rows(tm, scatter(dcur_ref, slot))

    @pl.when(i == n - 1)
    def _():
        _wait_rows(tm, scatter(dcur_ref, slot))


def _dispatch(dest_flat, xn, xb, tm, rows):
    n = xn.shape[0] // ROW_TILE
    any_spec = pl.BlockSpec(memory_space=pl.ANY)
    dest_spec = lambda off: pl.BlockSpec((TOP_K * tm,), lambda i: (jnp.maximum(i + off, 0),), memory_space=pltpu.SMEM)
    in_specs = [dest_spec(-1), dest_spec(0), any_spec]
    args = [dest_flat, dest_flat, xn]
    aliases = {}
    if xb is not None:
        in_specs.append(any_spec)
        args.append(xb)
        aliases = {3: 0}
    return pl.pallas_call(
        functools.partial(_dispatch_kernel, tm=tm),
        grid=(n // tm,),
        in_specs=in_specs,
        out_specs=any_spec,
        out_shape=jax.ShapeDtypeStruct((rows * ROW_TILE, LANES), F32),
        scratch_shapes=[pltpu.VMEM((2, tm * ROW_TILE, LANES), F32), pltpu.SemaphoreType.DMA((2,)),
                        pltpu.SemaphoreType.DMA((2,))],
        input_output_aliases=aliases,
        compiler_params=_cparams(1),
        name="dispatch",
    )(*args)


def _moe_kernel(be_ref, br_ref, nx_ref, sl_ref, nu_ref, xb_ref, wgu_hbm, bgu_ref, wd_hbm, bd_ref, y_ref,
                wgu_f32, wd_f32, wgu_bf, wd_bf, sems):
    i = pl.program_id(0)
    e = be_ref[i]
    slot = sl_ref[i]
    changed = jnp.logical_or(i == 0, e != be_ref[jnp.maximum(i - 1, 0)])

    def fetch(expert, s):
        return (pltpu.make_async_copy(wgu_hbm.at[expert], wgu_f32.at[s], sems.at[0, s]),
                pltpu.make_async_copy(wd_hbm.at[expert], wd_f32.at[s], sems.at[1, s]))

    @pl.when(i == 0)
    def _():
        for c in fetch(e, slot):
            c.start()

    @pl.when(changed)
    def _():
        @pl.when(nx_ref[i] >= 0)
        def _():
            for c in fetch(nx_ref[i], 1 - slot):
                c.start()

        for c in fetch(e, slot):
            c.wait()
        wgu_bf[...] = wgu_f32[slot].astype(BF16)
        wd_bf[...] = wd_f32[slot].astype(BF16)

    @pl.when(i < nu_ref[0])
    def _():
        xb = _load_row_tiles(xb_ref, EXPERT_ROWS).astype(BF16)
        gu = jnp.dot(xb, wgu_bf[...], preferred_element_type=F32) + bgu_ref[...]
        g = jnp.minimum(gu[:, :D_EXPERT], SWIGLU_LIMIT)
        u = jnp.clip(gu[:, D_EXPERT:], -SWIGLU_LIMIT, SWIGLU_LIMIT)
        act = (u + 1.0) * (g * jax.nn.sigmoid(SWIGLU_ALPHA * g))
        _store_row_tiles(y_ref, jnp.dot(act.astype(BF16), wd_bf[...], preferred_element_type=F32) + bd_ref[...])


def _moe(sched, xb, wgu, bgu, wd, bd):
    eb = EXPERT_ROWS
    block_e = sched[0]
    any_spec = pl.BlockSpec(memory_space=pl.ANY)
    grid_spec = pltpu.PrefetchScalarGridSpec(
        num_scalar_prefetch=len(sched),
        grid=(block_e.shape[0],),
        in_specs=[
            pl.BlockSpec((eb * ROW_TILE, LANES), lambda i, be, br, *_: (br[i], 0)),
            any_spec,
            pl.BlockSpec((None, 1, 2 * D_EXPERT), lambda i, be, *_: (be[i], 0, 0)),
            any_spec,
            pl.BlockSpec((None, 1, D_MODEL), lambda i, be, *_: (be[i], 0, 0)),
        ],
        out_specs=pl.BlockSpec((eb * ROW_TILE, LANES), lambda i, be, br, *_: (br[i], 0)),
        scratch_shapes=[pltpu.VMEM((2, D_MODEL, 2 * D_EXPERT), F32), pltpu.VMEM((2, D_EXPERT, D_MODEL), F32),
                        pltpu.VMEM((D_MODEL, 2 * D_EXPERT), BF16), pltpu.VMEM((D_EXPERT, D_MODEL), BF16),
                        pltpu.SemaphoreType.DMA((2, 2))],
    )
    return pl.pallas_call(
        _moe_kernel,
        grid_spec=grid_spec,
        out_shape=jax.ShapeDtypeStruct(xb.shape, F32),
        compiler_params=_cparams(1),
        name="moe",
    )(*sched, xb, wgu, bgu, wd, bd)


def _expert_blocks(counts, cap, n_assign):
    eb = EXPERT_ROWS
    n_grid = -(-n_assign // eb) + N_EXPERTS
    nblk = (counts + eb - 1) // eb
    blk_end = jnp.cumsum(nblk)
    n_used = blk_end[-1:].astype(jnp.int32)
    i = jnp.minimum(jnp.arange(n_grid, dtype=jnp.int32), n_used[0] - 1)
    block_e = jnp.sum(blk_end[None, :] <= i[:, None], axis=1).astype(jnp.int32)
    block_row = block_e * (cap // eb) + i - (blk_end - nblk)[block_e]
    ids = jnp.arange(N_EXPERTS, dtype=jnp.int32)
    used = nblk > 0
    later_used = jnp.logical_and(used[None, :], ids[None, :] > ids[:, None])
    next_e = jnp.min(jnp.where(later_used, ids[None, :], N_EXPERTS), axis=1)
    next_e = jnp.where(next_e < N_EXPERTS, next_e, -1).astype(jnp.int32)
    slot_e = ((jnp.cumsum(used.astype(jnp.int32)) - 1) % 2).astype(jnp.int32)
    return (block_e, block_row.astype(jnp.int32), next_e[block_e], slot_e[block_e], n_used)


def _ple_kernel(dcur_ref, dnext_ref, x1_ref, gate_ref, p_ref, gp_ref, wg_ref, wp_ref, yb_ref, y_ref, ybuf, sems):
    tm = x1_ref.shape[0]
    i = pl.program_id(0)
    slot = i % 2

    def gather(dest_ref, s):
        return lambda r: [_row_tile_copy(yb_ref, dest_ref[TOP_K * r + k], ybuf.at[s, k], r, sems.at[s])
                          for k in range(TOP_K)]

    @pl.when(i == 0)
    def _():
        _start_rows(tm, gather(dcur_ref, 0))

    @pl.when(i + 1 < pl.num_programs(0))
    def _():
        _start_rows(tm, gather(dnext_ref, 1 - slot))

    _wait_rows(tm, gather(dcur_ref, slot))

    gates = gate_ref[...]
    x2 = x1_ref[...]
    for k in range(TOP_K):
        x2 = x2 + _load_row_tiles(ybuf.at[slot, k], tm) * gates[:, k:k + 1]
    gate = jax.nn.sigmoid(jnp.dot(_rms(x2, gp_ref[...]).astype(BF16), wg_ref[...], preferred_element_type=F32))
    proj = jnp.dot(p_ref[...].astype(BF16), wp_ref[...], preferred_element_type=F32)
    y_ref[...] = x2 + gate * proj


def _ple(dest_flat, x1, gates, p2d, gp, wg, wp, yb, tm):
    n = x1.shape[0]
    row = lambda w: pl.BlockSpec((tm, w), lambda i: (i, 0))
    full = lambda shape: pl.BlockSpec(shape, lambda i: (0,) * len(shape))
    last = n // tm - 1
    dest_spec = lambda off: pl.BlockSpec((TOP_K * tm,), lambda i: (jnp.minimum(i + off, last),),
                                         memory_space=pltpu.SMEM)
    return pl.pallas_call(
        _ple_kernel,
        grid=(n // tm,),
        in_specs=[dest_spec(0), dest_spec(1),
                  row(D_MODEL), row(LANES), row(D_PLE), full((1, D_MODEL)),
                  full((D_MODEL, D_MODEL)), full((D_PLE, D_MODEL)), pl.BlockSpec(memory_space=pl.ANY)],
        out_specs=row(D_MODEL),
        out_shape=jax.ShapeDtypeStruct((n, D_MODEL), F32),
        scratch_shapes=[pltpu.VMEM((2, TOP_K, tm * ROW_TILE, LANES), F32), pltpu.SemaphoreType.DMA((2,))],
        compiler_params=_cparams(1),
        name="ple",
    )(dest_flat, dest_flat, x1, gates, p2d, gp, wg, wp, yb)


def kernel(x_prompt, x_sample, state_conv, state_h, cache_kv_w128, cache_kv_w512, cache_kv_w2048, p_prompt,
           p_sample, norm_mix_g, w_in, conv_w, conv_b, w_rg_a, b_rg_a, w_rg_x, b_rg_x, lru_lambda, q_norm_g,
           k_norm_g, rel_bias, w_br_rnn, w_br_attn, w_out, norm_ffn_g, w_router, b_router, w_gate_up, b_gate_up,
           w_down, b_down, norm_ple_g, w_ple_gate, w_ple_proj):
    bp, seq, _ = x_prompt.shape
    bs = x_sample.shape[0]
    n_p = bp * seq
    li = 0
    row = lambda v: v.reshape(1, -1).astype(F32)

    w = w_in[li]
    o_q, o_ga = 2 * D_RNN, 2 * D_RNN + 3 * D_QKV
    w_perm = jnp.concatenate([w[:, :o_q], w[:, o_ga:], w[:, o_q:o_ga]], axis=1).astype(BF16)
    colgain = jnp.concatenate([jnp.ones((COL_Q,), F32), jnp.tile(q_norm_g[li], N_ATTN_HEADS),
                               jnp.tile(k_norm_g[li], N_ATTN_HEADS), jnp.ones((D_QKV,), F32)]).reshape(1, D_IN)
    wax = jnp.concatenate([w_rg_a[li], w_rg_x[li]], axis=-1).astype(BF16)
    lru = (conv_w[li].astype(F32), row(conv_b[li]), wax, row(b_rg_a[li]), row(b_rg_x[li]), row(lru_lambda[li]))
    wr, wa, wo = w_br_rnn[li].astype(BF16), w_br_attn[li].astype(BF16), w_out[li].astype(BF16)
    wrt = jnp.pad(w_router[li].astype(BF16), ((0, 0), (0, LANES - N_EXPERTS)))
    brt = jnp.pad(row(b_router[li]), ((0, 0), (0, LANES - N_EXPERTS)))
    gf = row(norm_ffn_g[li])
    biases = [_group_bias(rel_bias, gi, wnd, dil) for gi, (wnd, dil) in enumerate(GROUPS)]

    xp = x_prompt.reshape(n_p, D_MODEL)
    zp = _inproj(xp, row(norm_mix_g[li]), w_perm, colgain, tm=2048, out_dtype=BF16)
    a_p, h_p = _rglru_prompt(zp, bp, seq, 512, *lru)
    os_p, ls_p = [], []
    for gi in range(len(GROUPS)):
        o, l = _attn_prompt(zp, _band_bias(biases[gi]), gi, bp, seq)
        os_p.append(o)
        ls_p.append(l)
    cap = -(-(n_p + bs) // EXPERT_ROWS) * EXPERT_ROWS
    x1_p, xn_p, gate_p, dest_p, cnt_p = _mix(a_p, os_p, ls_p, zp, xp, wr, wa, wo, gf, wrt, brt,
                                              jnp.zeros((1, LANES), jnp.int32), tm=256, cap=cap)

    xs = x_sample.reshape(bs, D_MODEL)
    zs = _inproj(xs, row(norm_mix_g[li]), w_perm, colgain, tm=bs, out_dtype=F32)
    sc = state_conv[li]
    a_s, h_s = _rglru_step(zs, [sc[:, j] for j in range(CONV_WIDTH - 1)], state_h[li], *lru)
    heads = lambda c0: zs[:, c0:c0 + D_QKV].astype(F32).reshape(bs, N_ATTN_HEADS, HEAD_DIM)
    caches = [c[li].reshape(bs, QUERY_BLOCK, dil, 2, HEADS_PER_GROUP, HEAD_DIM)
              for c, (_, dil) in zip((cache_kv_w128, cache_kv_w512, cache_kv_w2048), GROUPS)]
    bias_c = jnp.stack([b[:, :0:-1].T for b in biases])[..., None]
    bias_n = jnp.stack([b[:, 0] for b in biases])[..., None]
    o_s, l_s = _attn_step(heads(COL_Q), heads(COL_K), heads(COL_V), caches, bias_c, bias_n, bb=4)
    os_s = [o_s[g].reshape(bs, D_ATTN_OUT) for g in range(len(GROUPS))]
    ls_s = [l_s[g].reshape(bs, D_ATTN_OUT) for g in range(len(GROUPS))]
    x1_s, xn_s, gate_s, dest_s, cnt_s = _mix(a_s, os_s, ls_s, zs, xs, wr, wa, wo, gf, wrt, brt, cnt_p, tm=bs, cap=cap)

    dflat_p = dest_p[:, :TOP_K].reshape(-1)
    dflat_s = dest_s[:, :TOP_K].reshape(-1)
    xb = _dispatch(dflat_p, xn_p, None, 256, N_EXPERTS * cap)
    xb = _dispatch(dflat_s, xn_s, xb, bs, N_EXPERTS * cap)
    sched = _expert_blocks(cnt_s[0, :N_EXPERTS], cap, (n_p + bs) * TOP_K)
    yb = _moe(sched, xb, w_gate_up[li], b_gate_up[li][:, None, :], w_down[li], b_down[li][:, None, :])

    gp, wg, wp = row(norm_ple_g[li]), w_ple_gate[li].astype(BF16), w_ple_proj[li].astype(BF16)
    y_p = _ple(dflat_p, x1_p, gate_p, p_prompt[li].reshape(n_p, D_PLE), gp, wg, wp, yb, tm=256)
    y_s = _ple(dflat_s, x1_s, gate_s, p_sample[li].reshape(bs, D_PLE), gp, wg, wp, yb, tm=bs)

    z3 = zp.reshape(bp, seq, D_IN)
    zs3 = zs.reshape(bs, 1, D_IN)

    def new_kv(zz, gi, keep):
        t = zz.shape[1]
        c = gi * D_ATTN_OUT
        kk = zz[:, t - keep:, COL_K + c:COL_K + c + D_ATTN_OUT].astype(F32)
        vv = zz[:, t - keep:, COL_V + c:COL_V + c + D_ATTN_OUT].astype(F32)
        shape = (zz.shape[0], keep, HEADS_PER_GROUP, HEAD_DIM)
        return jnp.stack([kk.reshape(shape), vv.reshape(shape)], axis=2)[None]

    conv_p = z3[:, seq - (CONV_WIDTH - 1):, COL_XR:COL_XR + D_RNN].astype(F32)[None]
    conv_s = jnp.concatenate([sc[:, 1:], zs3[:, :, COL_XR:COL_XR + D_RNN].astype(F32)], axis=1)[None]
    outs = [y_p.reshape(bp, seq, D_MODEL), y_s.reshape(bs, 1, D_MODEL), conv_p, conv_s,
            h_p.reshape(1, bp, D_RNN), h_s.reshape(1, bs, D_RNN)]
    for gi, (wnd, _) in enumerate(GROUPS):
        outs.append(new_kv(z3, gi, min(wnd, seq)))
        outs.append(new_kv(zs3, gi, 1))
    return tuple(outs)
```

```python
import functools
import math

import numpy as np
import jax
import jax.numpy as jnp
from jax import lax
from jax.experimental import pallas as pl
from jax.experimental.pallas import tpu as pltpu

F32 = jnp.float32
BF16 = jnp.bfloat16

D_MODEL = 1024
D_RNN = 1024
N_RNN_BLOCKS = 8
RNN_BLOCK = D_RNN // N_RNN_BLOCKS
CONV_WIDTH = 4
LRU_C = 8.0
GROUPS = ((128, 1), (512, 4), (2048, 16))
HEADS_PER_GROUP = 4
HEAD_DIM = 128
N_ATTN_HEADS = HEADS_PER_GROUP * len(GROUPS)
D_QKV = N_ATTN_HEADS * HEAD_DIM
D_ATTN_OUT = HEADS_PER_GROUP * HEAD_DIM
N_BUCKETS = 32
MAX_DISTANCE = 2048
QUERY_BLOCK = 128
D_IN = 2 * D_RNN + 3 * D_QKV + 2 * D_MODEL
N_EXPERTS = 32
TOP_K = 4
D_EXPERT = 1024
SWIGLU_ALPHA = 1.702
SWIGLU_LIMIT = 7.0
D_PLE = 256
EPS = 1e-6
ATTN_SCALE = HEAD_DIM ** -0.5
MASK_VALUE = -1e30

LANES = 128
ROW_TILE = D_MODEL // LANES
COL_XR, COL_YG, COL_GA, COL_GB = 0, D_RNN, 2 * D_RNN, 2 * D_RNN + D_MODEL
COL_Q = 2 * D_RNN + 2 * D_MODEL
COL_K = COL_Q + D_QKV
COL_V = COL_K + D_QKV
INPROJ_TN = 512
INPROJ_SUB = 512
ATTN_SUPER = 2048
EXPERT_ROWS = 256
VMEM_LIMIT = 56 * 1024 * 1024


def _cparams(n_axes):
    return pltpu.CompilerParams(dimension_semantics=("arbitrary",) * n_axes, vmem_limit_bytes=VMEM_LIMIT)


def _rms(x, g):
    ms = jnp.mean(x * x, axis=-1, keepdims=True)
    return (x * lax.rsqrt(ms + EPS)) * g


def _round_bf16(x):
    return x.astype(BF16).astype(F32)


def _load_row_tiles(ref, n_rows):
    return jnp.concatenate([ref[pl.ds(c, n_rows, stride=ROW_TILE), :] for c in range(ROW_TILE)], axis=1)


def _store_row_tiles(ref, value):
    n_rows = value.shape[0]
    for c in range(ROW_TILE):
        ref[pl.ds(c, n_rows, stride=ROW_TILE), :] = value[:, c * LANES:(c + 1) * LANES]


def _row_tile_copy(src_ref, src_row, dst_ref, dst_row, sem):
    src = src_ref.at[pl.ds(pl.multiple_of(src_row * ROW_TILE, ROW_TILE), ROW_TILE)]
    dst = dst_ref.at[pl.ds(pl.multiple_of(dst_row * ROW_TILE, ROW_TILE), ROW_TILE)]
    return pltpu.make_async_copy(src, dst, sem)


def _inproj_kernel(x_ref, g_ref, w_ref, cg_ref, z_ref, u_scr, *, sub):
    j = pl.program_id(1)
    n_sub = x_ref.shape[0] // sub

    @pl.when(j == 0)
    def _():
        for s in range(n_sub):
            rows = slice(s * sub, (s + 1) * sub)
            u_scr[rows, :] = _rms(x_ref[rows, :], g_ref[...]).astype(BF16)

    is_qk = jnp.logical_and(j >= COL_Q // INPROJ_TN, j < COL_V // INPROJ_TN)
    for s in range(n_sub):
        rows = slice(s * sub, (s + 1) * sub)
        acc = jnp.dot(u_scr[rows, :], w_ref[...], preferred_element_type=F32)
        for h in range(INPROJ_TN // HEAD_DIM):
            cs = slice(h * HEAD_DIM, (h + 1) * HEAD_DIM)
            a = acc[:, cs]
            z_ref[rows, cs] = jnp.where(is_qk, _rms(a, cg_ref[:, cs]), a).astype(z_ref.dtype)


def _inproj(x2d, g, w_perm, colgain, tm, out_dtype):
    n = x2d.shape[0]
    return pl.pallas_call(
        functools.partial(_inproj_kernel, sub=min(tm, INPROJ_SUB)),
        grid=(n // tm, D_IN // INPROJ_TN),
        in_specs=[
            pl.BlockSpec((tm, D_MODEL), lambda i, j: (i, 0)),
            pl.BlockSpec((1, D_MODEL), lambda i, j: (0, 0)),
            pl.BlockSpec((D_MODEL, INPROJ_TN), lambda i, j: (0, j)),
            pl.BlockSpec((1, INPROJ_TN), lambda i, j: (0, j)),
        ],
        out_specs=pl.BlockSpec((tm, INPROJ_TN), lambda i, j: (i, j)),
        out_shape=jax.ShapeDtypeStruct((n, D_IN), out_dtype),
        scratch_shapes=[pltpu.VMEM((tm, D_MODEL), BF16)],
        compiler_params=_cparams(2),
        name="inproj",
    )(x2d, g, w_perm, colgain)


GELU_C1 = 2.0 * math.sqrt(2.0 / math.pi)
GELU_C2 = GELU_C1 * 0.044715


def _gelu_tanh(x):
    return x / (1.0 + jnp.exp(x * (-GELU_C1 - GELU_C2 * (x * x))))


def _softplus(v):
    return jnp.maximum(v, 0.0) + jnp.log1p(jnp.exp(-jnp.abs(v)))


def _lru_gates_block(xc, n, wax_ref, ba_ref, bx_ref, lam_ref):
    cs = slice(n * RNN_BLOCK, (n + 1) * RNN_BLOCK)
    ri = jnp.dot(xc.astype(BF16), wax_ref[n], preferred_element_type=F32)
    r = jax.nn.sigmoid(ri[:, :RNN_BLOCK] + ba_ref[:, cs])
    i = jax.nn.sigmoid(ri[:, RNN_BLOCK:] + bx_ref[:, cs])
    log_a = (-LRU_C * r) * _softplus(-lam_ref[:, cs])
    a = jnp.exp(log_a)
    v = 1.0 - a * a
    b = jnp.where(v > 0.0, v * lax.rsqrt(v), 0.0) * (i * xc)
    return a, b


def _rglru_kernel(xr_ref, yg_ref, cw_ref, cb_ref, wax_ref, ba_ref, bx_ref, lam_ref,
                  aout_ref, hlast_ref, xe_scr, a_scr, b_scr, hcar_scr, *, tc):
    c = pl.program_id(1)
    sub = 8

    @pl.when(c == 0)
    def _():
        xe_scr[0:8, :] = jnp.zeros((8, D_RNN), F32)
        hcar_scr[...] = jnp.zeros((sub, D_RNN), F32)

    xe_scr[8:8 + tc, :] = xr_ref[...].astype(F32)
    for n in range(N_RNN_BLOCKS):
        cs = slice(n * RNN_BLOCK, (n + 1) * RNN_BLOCK)
        xc = cb_ref[:, cs]
        for j in range(CONV_WIDTH):
            xc = xc + xe_scr[pl.ds(8 - (CONV_WIDTH - 1) + j, tc), cs] * cw_ref[j:j + 1, cs]
        a, b = _lru_gates_block(xc, n, wax_ref, ba_ref, bx_ref, lam_ref)
        a_scr[n] = a
        b_scr[n] = b
    xe_scr[0:8, :] = xe_scr[tc:tc + 8, :]

    row = lax.broadcasted_iota(jnp.int32, (sub, RNN_BLOCK), 0)

    def step(g, carry):
        rows = pl.ds(pl.multiple_of(g * sub, sub), sub)
        new = []
        for n in range(N_RNN_BLOCKS):
            a = a_scr[n, rows, :]
            b = b_scr[n, rows, :]
            for d in (1, 2, 4):
                keep = row >= d
                b = a * jnp.where(keep, pltpu.roll(b, d, 0), 0.0) + b
                a = a * jnp.where(keep, pltpu.roll(a, d, 0), 1.0)
            h = b + a * carry[n]
            b_scr[n, rows, :] = h
            new.append(jnp.broadcast_to(h[sub - 1:sub, :], (sub, RNN_BLOCK)))
        return tuple(new)

    init = tuple(hcar_scr[:, n * RNN_BLOCK:(n + 1) * RNN_BLOCK] for n in range(N_RNN_BLOCKS))
    h_in = lax.fori_loop(0, tc // sub, step, init)

    for n in range(N_RNN_BLOCKS):
        cs = slice(n * RNN_BLOCK, (n + 1) * RNN_BLOCK)
        aout_ref[:, cs] = (b_scr[n] * _gelu_tanh(yg_ref[:, cs].astype(F32))).astype(BF16)
        hcar_scr[:, cs] = h_in[n]
        hlast_ref[:, cs] = h_in[n][0:1, :]


def _rglru_prompt(z, batch, seq, tc, cw, cb, wax, ba, bx, lam):
    nc = seq // tc
    vec = lambda r: pl.BlockSpec((r, D_RNN), lambda b, c: (0, 0))
    return pl.pallas_call(
        functools.partial(_rglru_kernel, tc=tc),
        grid=(batch, nc),
        in_specs=[
            pl.BlockSpec((tc, D_RNN), lambda b, c: (b * nc + c, COL_XR // D_RNN)),
            pl.BlockSpec((tc, D_RNN), lambda b, c: (b * nc + c, COL_YG // D_RNN)),
            vec(CONV_WIDTH), vec(1),
            pl.BlockSpec((N_RNN_BLOCKS, RNN_BLOCK, 2 * RNN_BLOCK), lambda b, c: (0, 0, 0)),
            vec(1), vec(1), vec(1),
        ],
        out_specs=[
            pl.BlockSpec((tc, D_RNN), lambda b, c: (b * nc + c, 0)),
            pl.BlockSpec((None, 1, D_RNN), lambda b, c: (b, 0, 0)),
        ],
        out_shape=[jax.ShapeDtypeStruct((batch * seq, D_RNN), BF16),
                   jax.ShapeDtypeStruct((batch, 1, D_RNN), F32)],
        scratch_shapes=[pltpu.VMEM((tc + 8, D_RNN), F32)]
        + [pltpu.VMEM((N_RNN_BLOCKS, tc, RNN_BLOCK), F32)] * 2
        + [pltpu.VMEM((8, D_RNN), F32)],
        compiler_params=_cparams(2),
        name="rglru",
    )(z, z, cw, cb, wax, ba, bx, lam)


def _rglru_step_kernel(xr_ref, yg_ref, c0_ref, c1_ref, c2_ref, h0_ref, cw_ref, cb_ref, wax_ref, ba_ref,
                       bx_ref, lam_ref, aout_ref, hout_ref):
    taps = (c0_ref, c1_ref, c2_ref)
    for n in range(N_RNN_BLOCKS):
        cs = slice(n * RNN_BLOCK, (n + 1) * RNN_BLOCK)
        xc = cb_ref[:, cs]
        for j in range(CONV_WIDTH - 1):
            xc = xc + taps[j][:, cs] * cw_ref[j:j + 1, cs]
        xc = xc + xr_ref[:, cs].astype(F32) * cw_ref[CONV_WIDTH - 1:CONV_WIDTH, cs]
        a, b = _lru_gates_block(xc, n, wax_ref, ba_ref, bx_ref, lam_ref)
        h = a * h0_ref[:, cs] + b
        hout_ref[:, cs] = h
        aout_ref[:, cs] = (h * _gelu_tanh(yg_ref[:, cs].astype(F32))).astype(BF16)


def _rglru_step(z, conv_taps, h0, cw, cb, wax, ba, bx, lam):
    n = z.shape[0]
    full = lambda shape: pl.BlockSpec(shape, lambda i: (0,) * len(shape))
    return pl.pallas_call(
        _rglru_step_kernel,
        grid=(1,),
        in_specs=[
            pl.BlockSpec((n, D_RNN), lambda i: (0, COL_XR // D_RNN)),
            pl.BlockSpec((n, D_RNN), lambda i: (0, COL_YG // D_RNN)),
            full((n, D_RNN)), full((n, D_RNN)), full((n, D_RNN)), full((n, D_RNN)),
            full((CONV_WIDTH, D_RNN)), full((1, D_RNN)),
            full((N_RNN_BLOCKS, RNN_BLOCK, 2 * RNN_BLOCK)),
            full((1, D_RNN)), full((1, D_RNN)), full((1, D_RNN)),
        ],
        out_specs=[full((n, D_RNN)), full((n, D_RNN))],
        out_shape=[jax.ShapeDtypeStruct((n, D_RNN), BF16), jax.ShapeDtypeStruct((n, D_RNN), F32)],
        compiler_params=_cparams(1),
        name="rglru_step",
    )(z, z, *conv_taps, h0, cw, cb, wax, ba, bx, lam)


def _t5_causal_buckets(dist):
    max_exact = N_BUCKETS // 2
    d = np.maximum(dist, 1).astype(np.float32)
    large = max_exact + (np.log(d / max_exact) / np.log(MAX_DISTANCE / max_exact)
                         * (N_BUCKETS - max_exact)).astype(np.int32)
    large = np.minimum(large, N_BUCKETS - 1)
    return np.where(dist < max_exact, dist, large).astype(np.int32)


def _group_bias(rel_bias, gi, window, dilation):
    buckets = _t5_causal_buckets(np.arange(window // dilation + 1) * dilation)
    heads = slice(gi * HEADS_PER_GROUP, (gi + 1) * HEADS_PER_GROUP)
    return rel_bias[buckets][:, heads].T.astype(F32)


def _band_bias(bias):
    nh, nk = bias.shape
    qb, span = QUERY_BLOCK, 2 * QUERY_BLOCK
    assert nk == qb + 1
    w = jnp.concatenate([jnp.full((nh, qb - 1), MASK_VALUE, F32), bias[:, ::-1], jnp.full((nh, qb), MASK_VALUE, F32)],
                        axis=1)
    skew = jnp.tile(w, (1, qb + 1))[:, :qb * (w.shape[1] + 1)].reshape(nh, qb, w.shape[1] + 1)
    return skew[:, ::-1, :span]


def _attn_kernel(q_ref, k_ref, v_ref, bias_ref, o_ref, lse_ref, q_scr, kc_scr, vc_scr, o_scr, l_scr, *, d, sup):
    sb = pl.program_id(2)
    win = QUERY_BLOCK * d

    @pl.when(sb == 0)
    def _():
        kc_scr[0:win, :] = jnp.zeros((win, HEAD_DIM), F32)
        vc_scr[0:win, :] = jnp.zeros((win, HEAD_DIM), F32)

    q_scr[...] = q_ref[...].astype(F32)
    kc_scr[win:win + sup, :] = k_ref[...].astype(F32)
    vc_scr[win:win + sup, :] = v_ref[...].astype(F32)
    bias = bias_ref[...]
    col = lax.broadcasted_iota(jnp.int32, (QUERY_BLOCK, 2 * QUERY_BLOCK), 1)

    def rows(start, size):
        return pl.ds(start, size) if d == 1 else pl.ds(start, size, stride=d)

    for r in range(d):
        for qb in range(sup // win):
            start = r + win * qb
            qu = q_scr[rows(start, QUERY_BLOCK), :].astype(BF16)
            ku = kc_scr[rows(start, 2 * QUERY_BLOCK), :].astype(BF16)
            vu = vc_scr[rows(start, 2 * QUERY_BLOCK), :].astype(BF16)
            s = lax.dot_general(qu, ku, (((1,), (1,)), ((), ())), preferred_element_type=F32) * ATTN_SCALE + bias
            if qb == 0:
                s = jnp.where(jnp.logical_or(sb > 0, col >= QUERY_BLOCK), s, MASK_VALUE)
            m = jnp.max(s, axis=-1, keepdims=True)
            e = jnp.exp(s - m)
            den = jnp.sum(e, axis=-1, keepdims=True)
            o_scr[rows(start, QUERY_BLOCK), :] = jnp.dot((e / den).astype(BF16), vu, preferred_element_type=F32)
            l_scr[rows(start, QUERY_BLOCK), :] = jnp.broadcast_to(m + jnp.log(den), (QUERY_BLOCK, HEAD_DIM))

    o_ref[...] = o_scr[...].astype(BF16)
    lse_ref[...] = l_scr[...]
    kc_scr[0:win, :] = kc_scr[sup:sup + win, :]
    vc_scr[0:win, :] = vc_scr[sup:sup + win, :]


def _attn_prompt(z, band, gi, batch, seq):
    d = GROUPS[gi][1]
    sup = ATTN_SUPER
    win = QUERY_BLOCK * d
    nsb = seq // sup
    qb0 = COL_Q // HEAD_DIM + gi * HEADS_PER_GROUP
    kb0 = COL_K // HEAD_DIM + gi * HEADS_PER_GROUP
    vb0 = COL_V // HEAD_DIM + gi * HEADS_PER_GROUP
    blk = lambda c0: pl.BlockSpec((sup, HEAD_DIM), lambda b, h, s: (b * nsb + s, c0 + h))
    out = pl.BlockSpec((sup, HEAD_DIM), lambda b, h, s: (b * nsb + s, h))
    return pl.pallas_call(
        functools.partial(_attn_kernel, d=d, sup=sup),
        grid=(batch, HEADS_PER_GROUP, nsb),
        in_specs=[blk(qb0), blk(kb0), blk(vb0),
                  pl.BlockSpec((None, QUERY_BLOCK, 2 * QUERY_BLOCK), lambda b, h, s: (h, 0, 0))],
        out_specs=[out, out],
        out_shape=[jax.ShapeDtypeStruct((batch * seq, D_ATTN_OUT), BF16),
                   jax.ShapeDtypeStruct((batch * seq, D_ATTN_OUT), F32)],
        scratch_shapes=[pltpu.VMEM((sup, HEAD_DIM), F32), pltpu.VMEM((win + sup, HEAD_DIM), F32),
                        pltpu.VMEM((win + sup, HEAD_DIM), F32), pltpu.VMEM((sup, HEAD_DIM), F32),
                        pltpu.VMEM((sup, HEAD_DIM), F32)],
        compiler_params=_cparams(3),
        name=f"attn_g{gi}",
    )(z, z, z, band)


def _attn_step_kernel(q_ref, kn_ref, vn_ref, c0_ref, c1_ref, c2_ref, bc_ref, bn_ref, o_ref, lse_ref, *, bb):
    caches = (c0_ref, c1_ref, c2_ref)
    for b in range(bb):
        for g in range(len(GROUPS)):
            hs = slice(g * HEADS_PER_GROUP, (g + 1) * HEADS_PER_GROUP)
            qg = _round_bf16(q_ref[b, hs, :])
            kc = _round_bf16(caches[g][b, :, 0])
            vc = _round_bf16(caches[g][b, :, 1])
            kn = _round_bf16(kn_ref[b, hs, :])
            vn = _round_bf16(vn_ref[b, hs, :])
            sc = jnp.sum(kc * qg[None], axis=-1, keepdims=True) * ATTN_SCALE + bc_ref[g]
            sn = jnp.sum(kn * qg, axis=-1, keepdims=True) * ATTN_SCALE + bn_ref[g]
            m = jnp.maximum(jnp.max(sc, axis=0), sn)
            ec = jnp.exp(sc - m[None])
            en = jnp.exp(sn - m)
            den = jnp.sum(ec, axis=0) + en
            o = jnp.sum(_round_bf16(ec / den[None]) * vc, axis=0) + _round_bf16(en / den) * vn
            o_ref[g, b] = o
            lse_ref[g, b] = jnp.broadcast_to(m + jnp.log(den), (HEADS_PER_GROUP, HEAD_DIM))


def _attn_step(q, kn, vn, caches, bias_c, bias_n, bb):
    n = q.shape[0]
    ng = len(GROUPS)
    tok = pl.BlockSpec((bb, N_ATTN_HEADS, HEAD_DIM), lambda i: (i, 0, 0))
    cache = pl.BlockSpec((bb, QUERY_BLOCK, None, 2, HEADS_PER_GROUP, HEAD_DIM), lambda i: (i, 0, 0, 0, 0, 0))
    out = pl.BlockSpec((ng, bb, HEADS_PER_GROUP, HEAD_DIM), lambda i: (0, i, 0, 0))
    return pl.pallas_call(
        functools.partial(_attn_step_kernel, bb=bb),
        grid=(n // bb,),
        in_specs=[tok, tok, tok, cache, cache, cache,
                  pl.BlockSpec((ng, QUERY_BLOCK, HEADS_PER_GROUP, 1), lambda i: (0, 0, 0, 0)),
                  pl.BlockSpec((ng, HEADS_PER_GROUP, 1), lambda i: (0, 0, 0))],
        out_specs=[out, out],
        out_shape=[jax.ShapeDtypeStruct((ng, n, HEADS_PER_GROUP, HEAD_DIM), F32)] * 2,
        compiler_params=_cparams(1),
        name="attn_step",
    )(q, kn, vn, *caches, bias_c, bias_n)


def _mix_kernel(a_ref, o0_ref, o1_ref, o2_ref, l0_ref, l1_ref, l2_ref, ga_ref, gb_ref, x_ref,
                wr_ref, wa_ref, wo_ref, gf_ref, wrt_ref, brt_ref, cnt0_ref,
                x1_ref, xn_ref, gate_ref, dest_ref, cnt_ref, cnt_scr, *, cap, sub):

    @pl.when(pl.program_id(0) == 0)
    def _():
        cnt_scr[...] = cnt0_ref[...].astype(F32)

    for s in range(x_ref.shape[0] // sub):
        _mix_rows(slice(s * sub, (s + 1) * sub), a_ref, o0_ref, o1_ref, o2_ref, l0_ref, l1_ref, l2_ref, ga_ref,
                  gb_ref, x_ref, wr_ref, wa_ref, wo_ref, gf_ref, wrt_ref, brt_ref,
                  x1_ref, xn_ref, gate_ref, dest_ref, cnt_scr, cap)
    cnt_ref[...] = cnt_scr[...].astype(jnp.int32)


def _mix_rows(rows, a_ref, o0_ref, o1_ref, o2_ref, l0_ref, l1_ref, l2_ref, ga_ref, gb_ref, x_ref,
              wr_ref, wa_ref, wo_ref, gf_ref, wrt_ref, brt_ref, x1_ref, xn_ref, gate_ref, dest_ref, cnt_scr, cap):
    tm = rows.stop - rows.start
    l0, l1, l2 = l0_ref[rows, :], l1_ref[rows, :], l2_ref[rows, :]
    mx = jnp.maximum(jnp.maximum(l0, l1), l2)
    e0, e1, e2 = jnp.exp(l0 - mx), jnp.exp(l1 - mx), jnp.exp(l2 - mx)
    den = e0 + e1 + e2
    b_out = (_round_bf16(e0 / den) * _round_bf16(o0_ref[rows, :]) + _round_bf16(e1 / den) * _round_bf16(o1_ref[rows, :])
             + _round_bf16(e2 / den) * _round_bf16(o2_ref[rows, :]))
    rnn = jnp.dot(a_ref[rows, :], wr_ref[...], preferred_element_type=F32)
    att = jnp.dot(b_out.astype(BF16), wa_ref[...], preferred_element_type=F32)
    merged = (jax.nn.sigmoid(ga_ref[rows, :].astype(F32)) * rnn
              + jax.nn.sigmoid(gb_ref[rows, :].astype(F32)) * att)
    x1 = x_ref[rows, :] + jnp.dot(merged.astype(BF16), wo_ref[...], preferred_element_type=F32)
    x1_ref[rows, :] = x1
    xn = _rms(x1, gf_ref[...])
    _store_row_tiles(xn_ref.at[pl.ds(rows.start * ROW_TILE, tm * ROW_TILE)], xn)

    logits = jnp.dot(xn.astype(BF16), wrt_ref[...], preferred_element_type=F32) + brt_ref[...]
    lane = lax.broadcasted_iota(jnp.int32, (tm, LANES), 1)
    cur = jnp.where(lane < N_EXPERTS, logits, -jnp.inf)
    vals, idxs = [], []
    for _ in range(TOP_K):
        mk = jnp.max(cur, axis=-1, keepdims=True)
        ik = jnp.min(jnp.where(cur == mk, lane, LANES), axis=-1, keepdims=True)
        vals.append(mk)
        idxs.append(ik)
        cur = jnp.where(lane == ik, -jnp.inf, cur)
    es = [jnp.exp(v - vals[0]) for v in vals]
    den = es[0] + es[1] + es[2] + es[3]
    chosen = jnp.zeros((tm, LANES), F32)
    for k in range(TOP_K):
        chosen = chosen + jnp.where(lane == idxs[k], 1.0, 0.0)
    tri = jnp.where(lax.broadcasted_iota(jnp.int32, (tm, tm), 1) <= lax.broadcasted_iota(jnp.int32, (tm, tm), 0),
                    1.0, 0.0).astype(BF16)
    incl = jnp.dot(tri, chosen.astype(BF16), preferred_element_type=F32)
    rank = cnt_scr[...] + incl - chosen
    gate_out = jnp.zeros((tm, LANES), F32)
    dest_out = jnp.zeros((tm, LANES), jnp.int32)
    for k in range(TOP_K):
        rank_k = jnp.sum(jnp.where(lane == idxs[k], rank, 0.0), axis=-1, keepdims=True).astype(jnp.int32)
        gate_out = jnp.where(lane == k, es[k] / den, gate_out)
        dest_out = jnp.where(lane == k, idxs[k] * cap + rank_k, dest_out)
    gate_ref[rows, :] = gate_out
    if tm % LANES:
        dest_out = jnp.concatenate([dest_out, jnp.zeros((LANES - tm % LANES, LANES), jnp.int32)], axis=0)
    dest_ref[:, rows] = dest_out.T[0:DEST_ROWS, 0:tm]
    cnt_scr[...] = cnt_scr[...] + incl[tm - 1:tm, :]


def _mix(a_out, os_, ls_, z, x2d, wr, wa, wo, gf, wrt, brt, cnt0, tm, cap):
    n = x2d.shape[0]
    row = lambda w, c=0: pl.BlockSpec((tm, w), lambda i: (i, c))
    full = lambda shape: pl.BlockSpec(shape, lambda i: (0,) * len(shape))
    return pl.pallas_call(
        functools.partial(_mix_kernel, cap=cap, sub=min(tm, MIX_SUB)),
        grid=(n // tm,),
        in_specs=[row(D_RNN), row(D_ATTN_OUT), row(D_ATTN_OUT), row(D_ATTN_OUT),
                  row(D_ATTN_OUT), row(D_ATTN_OUT), row(D_ATTN_OUT),
                  row(D_MODEL, COL_GA // D_MODEL), row(D_MODEL, COL_GB // D_MODEL), row(D_MODEL),
                  full((D_RNN, D_MODEL)), full((D_ATTN_OUT, D_MODEL)), full((D_MODEL, D_MODEL)),
                  full((1, D_MODEL)), full((D_MODEL, LANES)), full((1, LANES)), full((1, LANES))],
        out_specs=[row(D_MODEL), pl.BlockSpec((tm * ROW_TILE, LANES), lambda i: (i, 0)), row(LANES),
                   pl.BlockSpec((DEST_ROWS, tm), lambda i: (0, i)), full((1, LANES))],
        out_shape=[jax.ShapeDtypeStruct((n, D_MODEL), F32), jax.ShapeDtypeStruct((n * ROW_TILE, LANES), F32),
                   jax.ShapeDtypeStruct((n, LANES), F32), jax.ShapeDtypeStruct((DEST_ROWS, n), jnp.int32),
                   jax.ShapeDtypeStruct((1, LANES), jnp.int32)],
        scratch_shapes=[pltpu.VMEM((1, LANES), F32)],
        compiler_params=_cparams(1),
        name="mix",
    )(a_out, *os_, *ls_, z, z, x2d, wr, wa, wo, gf, wrt, brt, cnt0)


DMA_UNROLL = 8
DEST_ROWS = 8
MIX_SUB = 256


def _start_rows(tm, copies):
    def body(r, carry):
        for k, c in enumerate(copies(r)):
            c.start(priority=k % 2)
        return carry

    lax.fori_loop(0, tm, body, 0, unroll=DMA_UNROLL)


def _wait_rows(tm, copies):
    first = copies(0)

    def body(r, carry):
        for c in first:
            c.wait()
        return carry

    lax.fori_loop(0, tm, body, 0, unroll=DMA_UNROLL)


def _dispatch_kernel(dprev_ref, dcur_ref, xn_hbm, *rest, tm):
    xb_ref, src, load_sem, row_sem = rest[-4:]
    i = pl.program_id(0)
    n = pl.num_programs(0)
    slot = i % 2

    def load(tile, s):
        rows = pl.ds(pl.multiple_of(tile * (tm * ROW_TILE), ROW_TILE), tm * ROW_TILE)
        return pltpu.make_async_copy(xn_hbm.at[rows], src.at[s], load_sem.at[s])

    def scatter(dest_ref, s):
        return lambda r: [_row_tile_copy(src.at[s], r, xb_ref, dest_ref[k, r], row_sem.at[s])
                          for k in range(TOP_K)]

    @pl.when(i == 0)
    def _():
        load(0, 0).start()

    @pl.when(i > 0)
    def _():
        _wait_rows(tm, scatter(dprev_ref, 1 - slot))

    @pl.when(i + 1 < n)
    def _():
        load(i + 1, 1 - slot).start()

    load(i, slot).wait()
    _start_rows(tm, scatter(dcur_ref, slot))

    @pl.when(i == n - 1)
    def _():
        _wait_rows(tm, scatter(dcur_ref, slot))


def _dispatch(dest_flat, xn, xb, tm, rows):
    n = xn.shape[0] // ROW_TILE
    any_spec = pl.BlockSpec(memory_space=pl.ANY)
    dest_spec = lambda off: pl.BlockSpec((DEST_ROWS, tm), lambda i: (0, jnp.maximum(i + off, 0)),
                                         memory_space=pltpu.SMEM)
    in_specs = [dest_spec(-1), dest_spec(0), any_spec]
    args = [dest_flat, dest_flat, xn]
    aliases = {}
    if xb is not None:
        in_specs.append(any_spec)
        args.append(xb)
        aliases = {3: 0}
    return pl.pallas_call(
        functools.partial(_dispatch_kernel, tm=tm),
        grid=(n // tm,),
        in_specs=in_specs,
        out_specs=any_spec,
        out_shape=jax.ShapeDtypeStruct((rows * ROW_TILE, LANES), F32),
        scratch_shapes=[pltpu.VMEM((2, tm * ROW_TILE, LANES), F32), pltpu.SemaphoreType.DMA((2,)),
                        pltpu.SemaphoreType.DMA((2,))],
        input_output_aliases=aliases,
        compiler_params=_cparams(1),
        name="dispatch",
    )(*args)


def _moe_kernel(be_ref, br_ref, nx_ref, sl_ref, nu_ref, xb_ref, wgu_hbm, bgu_ref, wd_hbm, bd_ref, y_ref,
                wgu_f32, wd_f32, wgu_bf, wd_bf, sems):
    i = pl.program_id(0)
    e = be_ref[i]
    slot = sl_ref[i]
    changed = jnp.logical_or(i == 0, e != be_ref[jnp.maximum(i - 1, 0)])

    def fetch(expert, s):
        return (pltpu.make_async_copy(wgu_hbm.at[expert], wgu_f32.at[s], sems.at[0, s]),
                pltpu.make_async_copy(wd_hbm.at[expert], wd_f32.at[s], sems.at[1, s]))

    @pl.when(i == 0)
    def _():
        for c in fetch(e, slot):
            c.start()

    @pl.when(changed)
    def _():
        @pl.when(nx_ref[i] >= 0)
        def _():
            for c in fetch(nx_ref[i], 1 - slot):
                c.start(priority=1)

        for c in fetch(e, slot):
            c.wait()
        wgu_bf[...] = wgu_f32[slot].astype(BF16)
        wd_bf[...] = wd_f32[slot].astype(BF16)

    @pl.when(i < nu_ref[0])
    def _():
        xb = _load_row_tiles(xb_ref, EXPERT_ROWS).astype(BF16)
        gu = jnp.dot(xb, wgu_bf[...], preferred_element_type=F32) + bgu_ref[...]
        g = jnp.minimum(gu[:, :D_EXPERT], SWIGLU_LIMIT)
        u = jnp.clip(gu[:, D_EXPERT:], -SWIGLU_LIMIT, SWIGLU_LIMIT)
        act = (u + 1.0) * (g * jax.nn.sigmoid(SWIGLU_ALPHA * g))
        _store_row_tiles(y_ref, jnp.dot(act.astype(BF16), wd_bf[...], preferred_element_type=F32) + bd_ref[...])


def _moe(sched, xb, wgu, bgu, wd, bd):
    eb = EXPERT_ROWS
    block_e = sched[0]
    any_spec = pl.BlockSpec(memory_space=pl.ANY)
    grid_spec = pltpu.PrefetchScalarGridSpec(
        num_scalar_prefetch=len(sched),
        grid=(block_e.shape[0],),
        in_specs=[
            pl.BlockSpec((eb * ROW_TILE, LANES), lambda i, be, br, *_: (br[i], 0)),
            any_spec,
            pl.BlockSpec((None, 1, 2 * D_EXPERT), lambda i, be, *_: (be[i], 0, 0)),
            any_spec,
            pl.BlockSpec((None, 1, D_MODEL), lambda i, be, *_: (be[i], 0, 0)),
        ],
        out_specs=pl.BlockSpec((eb * ROW_TILE, LANES), lambda i, be, br, *_: (br[i], 0)),
        scratch_shapes=[pltpu.VMEM((2, D_MODEL, 2 * D_EXPERT), F32), pltpu.VMEM((2, D_EXPERT, D_MODEL), F32),
                        pltpu.VMEM((D_MODEL, 2 * D_EXPERT), BF16), pltpu.VMEM((D_EXPERT, D_MODEL), BF16),
                        pltpu.SemaphoreType.DMA((2, 2))],
    )
    return pl.pallas_call(
        _moe_kernel,
        grid_spec=grid_spec,
        out_shape=jax.ShapeDtypeStruct(xb.shape, F32),
        compiler_params=_cparams(1),
        name="moe",
    )(*sched, xb, wgu, bgu, wd, bd)


def _expert_blocks(counts, cap, n_assign):
    eb = EXPERT_ROWS
    n_grid = -(-n_assign // eb) + N_EXPERTS
    nblk = (counts + eb - 1) // eb
    blk_end = jnp.cumsum(nblk)
    n_used = blk_end[-1:].astype(jnp.int32)
    i = jnp.minimum(jnp.arange(n_grid, dtype=jnp.int32), n_used[0] - 1)
    block_e = jnp.sum(blk_end[None, :] <= i[:, None], axis=1).astype(jnp.int32)
    block_row = block_e * (cap // eb) + i - (blk_end - nblk)[block_e]
    ids = jnp.arange(N_EXPERTS, dtype=jnp.int32)
    used = nblk > 0
    later_used = jnp.logical_and(used[None, :], ids[None, :] > ids[:, None])
    next_e = jnp.min(jnp.where(later_used, ids[None, :], N_EXPERTS), axis=1)
    next_e = jnp.where(next_e < N_EXPERTS, next_e, -1).astype(jnp.int32)
    slot_e = ((jnp.cumsum(used.astype(jnp.int32)) - 1) % 2).astype(jnp.int32)
    return (block_e, block_row.astype(jnp.int32), next_e[block_e], slot_e[block_e], n_used)


def _ple_kernel(dcur_ref, dnext_ref, x1_ref, gate_ref, p_ref, gp_ref, wg_ref, wp_ref, yb_ref, y_ref, ybuf, sems):
    tm = x1_ref.shape[0]
    i = pl.program_id(0)
    slot = i % 2

    def gather(dest_ref, s):
        return lambda r: [_row_tile_copy(yb_ref, dest_ref[k, r], ybuf.at[s, k], r, sems.at[s])
                          for k in range(TOP_K)]

    @pl.when(i == 0)
    def _():
        _start_rows(tm, gather(dcur_ref, 0))

    @pl.when(i + 1 < pl.num_programs(0))
    def _():
        _start_rows(tm, gather(dnext_ref, 1 - slot))

    _wait_rows(tm, gather(dcur_ref, slot))

    gates = gate_ref[...]
    x2 = x1_ref[...]
    for k in range(TOP_K):
        x2 = x2 + _load_row_tiles(ybuf.at[slot, k], tm) * gates[:, k:k + 1]
    gate = jax.nn.sigmoid(jnp.dot(_rms(x2, gp_ref[...]).astype(BF16), wg_ref[...], preferred_element_type=F32))
    proj = jnp.dot(p_ref[...].astype(BF16), wp_ref[...], preferred_element_type=F32)
    y_ref[...] = x2 + gate * proj


def _ple(dest_flat, x1, gates, p2d, gp, wg, wp, yb, tm):
    n = x1.shape[0]
    row = lambda w: pl.BlockSpec((tm, w), lambda i: (i, 0))
    full = lambda shape: pl.BlockSpec(shape, lambda i: (0,) * len(shape))
    last = n // tm - 1
    dest_spec = lambda off: pl.BlockSpec((DEST_ROWS, tm), lambda i: (0, jnp.minimum(i + off, last)),
                                         memory_space=pltpu.SMEM)
    return pl.pallas_call(
        _ple_kernel,
        grid=(n // tm,),
        in_specs=[dest_spec(0), dest_spec(1),
                  row(D_MODEL), row(LANES), row(D_PLE), full((1, D_MODEL)),
                  full((D_MODEL, D_MODEL)), full((D_PLE, D_MODEL)), pl.BlockSpec(memory_space=pl.ANY)],
        out_specs=row(D_MODEL),
        out_shape=jax.ShapeDtypeStruct((n, D_MODEL), F32),
        scratch_shapes=[pltpu.VMEM((2, TOP_K, tm * ROW_TILE, LANES), F32), pltpu.SemaphoreType.DMA((2,))],
        compiler_params=_cparams(1),
        name="ple",
    )(dest_flat, dest_flat, x1, gates, p2d, gp, wg, wp, yb)


def kernel(x_prompt, x_sample, state_conv, state_h, cache_kv_w128, cache_kv_w512, cache_kv_w2048, p_prompt,
           p_sample, norm_mix_g, w_in, conv_w, conv_b, w_rg_a, b_rg_a, w_rg_x, b_rg_x, lru_lambda, q_norm_g,
           k_norm_g, rel_bias, w_br_rnn, w_br_attn, w_out, norm_ffn_g, w_router, b_router, w_gate_up, b_gate_up,
           w_down, b_down, norm_ple_g, w_ple_gate, w_ple_proj):
    bp, seq, _ = x_prompt.shape
    bs = x_sample.shape[0]
    n_p = bp * seq
    li = 0
    row = lambda v: v.reshape(1, -1).astype(F32)

    w = w_in[li]
    o_q, o_ga = 2 * D_RNN, 2 * D_RNN + 3 * D_QKV
    w_perm = jnp.concatenate([w[:, :o_q], w[:, o_ga:], w[:, o_q:o_ga]], axis=1).astype(BF16)
    colgain = jnp.concatenate([jnp.ones((COL_Q,), F32), jnp.tile(q_norm_g[li], N_ATTN_HEADS),
                               jnp.tile(k_norm_g[li], N_ATTN_HEADS), jnp.ones((D_QKV,), F32)]).reshape(1, D_IN)
    wax = jnp.concatenate([w_rg_a[li], w_rg_x[li]], axis=-1).astype(BF16)
    lru = (conv_w[li].astype(F32), row(conv_b[li]), wax, row(b_rg_a[li]), row(b_rg_x[li]), row(lru_lambda[li]))
    wr, wa, wo = w_br_rnn[li].astype(BF16), w_br_attn[li].astype(BF16), w_out[li].astype(BF16)
    wrt = jnp.pad(w_router[li].astype(BF16), ((0, 0), (0, LANES - N_EXPERTS)))
    brt = jnp.pad(row(b_router[li]), ((0, 0), (0, LANES - N_EXPERTS)))
    gf = row(norm_ffn_g[li])
    biases = [_group_bias(rel_bias, gi, wnd, dil) for gi, (wnd, dil) in enumerate(GROUPS)]

    xp = x_prompt.reshape(n_p, D_MODEL)
    zp = _inproj(xp, row(norm_mix_g[li]), w_perm, colgain, tm=2048, out_dtype=BF16)
    a_p, h_p = _rglru_prompt(zp, bp, seq, 512, *lru)
    os_p, ls_p = [], []
    for gi in range(len(GROUPS)):
        o, l = _attn_prompt(zp, _band_bias(biases[gi]), gi, bp, seq)
        os_p.append(o)
        ls_p.append(l)
    cap = -(-(n_p + bs) // EXPERT_ROWS) * EXPERT_ROWS
    x1_p, xn_p, gate_p, dest_p, cnt_p = _mix(a_p, os_p, ls_p, zp, xp, wr, wa, wo, gf, wrt, brt,
                                              jnp.zeros((1, LANES), jnp.int32), tm=512, cap=cap)

    xs = x_sample.reshape(bs, D_MODEL)
    zs = _inproj(xs, row(norm_mix_g[li]), w_perm, colgain, tm=bs, out_dtype=F32)
    sc = state_conv[li]
    a_s, h_s = _rglru_step(zs, [sc[:, j] for j in range(CONV_WIDTH - 1)], state_h[li], *lru)
    heads = lambda c0: zs[:, c0:c0 + D_QKV].astype(F32).reshape(bs, N_ATTN_HEADS, HEAD_DIM)
    caches = [c[li].reshape(bs, QUERY_BLOCK, dil, 2, HEADS_PER_GROUP, HEAD_DIM)
              for c, (_, dil) in zip((cache_kv_w128, cache_kv_w512, cache_kv_w2048), GROUPS)]
    bias_c = jnp.stack([b[:, :0:-1].T for b in biases])[..., None]
    bias_n = jnp.stack([b[:, 0] for b in biases])[..., None]
    o_s, l_s = _attn_step(heads(COL_Q), heads(COL_K), heads(COL_V), caches, bias_c, bias_n, bb=4)
    os_s = [o_s[g].reshape(bs, D_ATTN_OUT) for g in range(len(GROUPS))]
    ls_s = [l_s[g].reshape(bs, D_ATTN_OUT) for g in range(len(GROUPS))]
    x1_s, xn_s, gate_s, dest_s, cnt_s = _mix(a_s, os_s, ls_s, zs, xs, wr, wa, wo, gf, wrt, brt, cnt_p, tm=bs, cap=cap)

    xb = _dispatch(dest_p, xn_p, None, 256, N_EXPERTS * cap)
    xb = _dispatch(dest_s, xn_s, xb, bs, N_EXPERTS * cap)
    sched = _expert_blocks(cnt_s[0, :N_EXPERTS], cap, (n_p + bs) * TOP_K)
    yb = _moe(sched, xb, w_gate_up[li], b_gate_up[li][:, None, :], w_down[li], b_down[li][:, None, :])

    gp, wg, wp = row(norm_ple_g[li]), w_ple_gate[li].astype(BF16), w_ple_proj[li].astype(BF16)
    y_p = _ple(dest_p, x1_p, gate_p, p_prompt[li].reshape(n_p, D_PLE), gp, wg, wp, yb, tm=256)
    y_s = _ple(dest_s, x1_s, gate_s, p_sample[li].reshape(bs, D_PLE), gp, wg, wp, yb, tm=bs)

    z3 = zp.reshape(bp, seq, D_IN)
    zs3 = zs.reshape(bs, 1, D_IN)

    def new_kv(zz, gi, keep):
        t = zz.shape[1]
        c = gi * D_ATTN_OUT
        kk = zz[:, t - keep:, COL_K + c:COL_K + c + D_ATTN_OUT].astype(F32)
        vv = zz[:, t - keep:, COL_V + c:COL_V + c + D_ATTN_OUT].astype(F32)
        shape = (zz.shape[0], keep, HEADS_PER_GROUP, HEAD_DIM)
        return jnp.stack([kk.reshape(shape), vv.reshape(shape)], axis=2)[None]

    conv_p = z3[:, seq - (CONV_WIDTH - 1):, COL_XR:COL_XR + D_RNN].astype(F32)[None]
    conv_s = jnp.concatenate([sc[:, 1:], zs3[:, :, COL_XR:COL_XR + D_RNN].astype(F32)], axis=1)[None]
    outs = [y_p.reshape(bp, seq, D_MODEL), y_s.reshape(bs, 1, D_MODEL), conv_p, conv_s,
            h_p.reshape(1, bp, D_RNN), h_s.reshape(1, bs, D_RNN)]
    for gi, (wnd, _) in enumerate(GROUPS):
        outs.append(new_kv(z3, gi, min(wnd, seq)))
        outs.append(new_kv(zs3, gi, 1))
    return tuple(outs)
```

```python
import functools
import math

import numpy as np
import jax
import jax.numpy as jnp
from jax import lax
from jax.experimental import pallas as pl
from jax.experimental.pallas import tpu as pltpu

F32 = jnp.float32
BF16 = jnp.bfloat16

D_MODEL = 1024
D_RNN = 1024
N_RNN_BLOCKS = 8
RNN_BLOCK = D_RNN // N_RNN_BLOCKS
CONV_WIDTH = 4
LRU_C = 8.0
GROUPS = ((128, 1), (512, 4), (2048, 16))
HEADS_PER_GROUP = 4
HEAD_DIM = 128
N_ATTN_HEADS = HEADS_PER_GROUP * len(GROUPS)
D_QKV = N_ATTN_HEADS * HEAD_DIM
D_ATTN_OUT = HEADS_PER_GROUP * HEAD_DIM
N_BUCKETS = 32
MAX_DISTANCE = 2048
QUERY_BLOCK = 128
D_IN = 2 * D_RNN + 3 * D_QKV + 2 * D_MODEL
N_EXPERTS = 32
TOP_K = 4
D_EXPERT = 1024
SWIGLU_ALPHA = 1.702
SWIGLU_LIMIT = 7.0
D_PLE = 256
EPS = 1e-6
ATTN_SCALE = HEAD_DIM ** -0.5
MASK_VALUE = -1e30

LANES = 128
ROW_TILE = D_MODEL // LANES
COL_XR, COL_YG, COL_GA, COL_GB = 0, D_RNN, 2 * D_RNN, 2 * D_RNN + D_MODEL
COL_Q = 2 * D_RNN + 2 * D_MODEL
COL_K = COL_Q + D_QKV
COL_V = COL_K + D_QKV
INPROJ_TN = D_IN // 4
INPROJ_CHUNKS = ((0, 512), (512, 512), (1024, 512), (1536, 640))
INPROJ_SUB = 512
ATTN_SUPER = 2048
BAND_LANES = 512
EXPERT_ROWS = 512
VMEM_LIMIT = 56 * 1024 * 1024


def _cparams(n_axes):
    return pltpu.CompilerParams(dimension_semantics=("arbitrary",) * n_axes, vmem_limit_bytes=VMEM_LIMIT)


def _rms(x, g):
    ms = jnp.mean(x * x, axis=-1, keepdims=True)
    return (x * lax.rsqrt(ms + EPS)) * g


def _round_bf16(x):
    return x.astype(BF16).astype(F32)


def _load_row_tiles(ref, n_rows):
    return jnp.concatenate([ref[pl.ds(c, n_rows, stride=ROW_TILE), :] for c in range(ROW_TILE)], axis=1)


def _store_row_tiles(ref, value):
    n_rows = value.shape[0]
    for c in range(ROW_TILE):
        ref[pl.ds(c, n_rows, stride=ROW_TILE), :] = value[:, c * LANES:(c + 1) * LANES]


def _row_tile_copy(src_ref, src_row, dst_ref, dst_row, sem):
    src = src_ref.at[pl.ds(pl.multiple_of(src_row * ROW_TILE, ROW_TILE), ROW_TILE)]
    dst = dst_ref.at[pl.ds(pl.multiple_of(dst_row * ROW_TILE, ROW_TILE), ROW_TILE)]
    return pltpu.make_async_copy(src, dst, sem)


def _inproj_kernel(x_ref, g_ref, w_ref, cg_ref, z_ref, u_scr, *, sub):
    j = pl.program_id(1)
    n_sub = x_ref.shape[0] // sub

    @pl.when(j == 0)
    def _():
        for s in range(n_sub):
            rows = slice(s * sub, (s + 1) * sub)
            u_scr[rows, :] = _rms(x_ref[rows, :], g_ref[...]).astype(BF16)

    for s in range(n_sub):
        rows = slice(s * sub, (s + 1) * sub)
        for c0, width in INPROJ_CHUNKS:
            acc = jnp.dot(u_scr[rows, :], w_ref[:, c0:c0 + width], preferred_element_type=F32)
            for h in range(width // HEAD_DIM):
                cs = slice(c0 + h * HEAD_DIM, c0 + (h + 1) * HEAD_DIM)
                col = j * INPROJ_TN + cs.start
                is_qk = jnp.logical_and(col >= COL_Q, col < COL_V)
                a = acc[:, h * HEAD_DIM:(h + 1) * HEAD_DIM]
                z_ref[rows, cs] = jnp.where(is_qk, _rms(a, cg_ref[:, cs]), a).astype(z_ref.dtype)


def _inproj(x2d, g, w_perm, colgain, tm, out_dtype):
    n = x2d.shape[0]
    return pl.pallas_call(
        functools.partial(_inproj_kernel, sub=min(tm, INPROJ_SUB)),
        grid=(n // tm, D_IN // INPROJ_TN),
        in_specs=[
            pl.BlockSpec((tm, D_MODEL), lambda i, j: (i, 0)),
            pl.BlockSpec((1, D_MODEL), lambda i, j: (0, 0)),
            pl.BlockSpec((D_MODEL, INPROJ_TN), lambda i, j: (0, j)),
            pl.BlockSpec((1, INPROJ_TN), lambda i, j: (0, j)),
        ],
        out_specs=pl.BlockSpec((tm, INPROJ_TN), lambda i, j: (i, j)),
        out_shape=jax.ShapeDtypeStruct((n, D_IN), out_dtype),
        scratch_shapes=[pltpu.VMEM((tm, D_MODEL), BF16)],
        compiler_params=_cparams(2),
        name="inproj",
    )(x2d, g, w_perm, colgain)


GELU_C1 = 2.0 * math.sqrt(2.0 / math.pi)
GELU_C2 = GELU_C1 * 0.044715


def _gelu_tanh(x):
    return x / (1.0 + jnp.exp(x * (-GELU_C1 - GELU_C2 * (x * x))))


def _softplus(v):
    return jnp.maximum(v, 0.0) + jnp.log1p(jnp.exp(-jnp.abs(v)))


def _lru_gates_block(xc, n, wax_ref, ba_ref, bx_ref, lam_ref):
    cs = slice(n * RNN_BLOCK, (n + 1) * RNN_BLOCK)
    ri = jnp.dot(xc.astype(BF16), wax_ref[n], preferred_element_type=F32)
    r = jax.nn.sigmoid(ri[:, :RNN_BLOCK] + ba_ref[:, cs])
    i = jax.nn.sigmoid(ri[:, RNN_BLOCK:] + bx_ref[:, cs])
    log_a = (-LRU_C * r) * _softplus(-lam_ref[:, cs])
    a = jnp.exp(log_a)
    v = 1.0 - a * a
    b = jnp.where(v > 0.0, v * lax.rsqrt(v), 0.0) * (i * xc)
    return a, b


def _rglru_kernel(xr_ref, yg_ref, cw_ref, cb_ref, wax_ref, ba_ref, bx_ref, lam_ref,
                  aout_ref, hlast_ref, xe_scr, a_scr, b_scr, hcar_scr, *, tc):
    c = pl.program_id(1)
    sub = 8

    @pl.when(c == 0)
    def _():
        xe_scr[0:8, :] = jnp.zeros((8, D_RNN), F32)
        hcar_scr[...] = jnp.zeros((sub, D_RNN), F32)

    xe_scr[8:8 + tc, :] = xr_ref[...].astype(F32)
    for n in range(N_RNN_BLOCKS):
        cs = slice(n * RNN_BLOCK, (n + 1) * RNN_BLOCK)
        xc = cb_ref[:, cs]
        for j in range(CONV_WIDTH):
            xc = xc + xe_scr[pl.ds(8 - (CONV_WIDTH - 1) + j, tc), cs] * cw_ref[j:j + 1, cs]
        a, b = _lru_gates_block(xc, n, wax_ref, ba_ref, bx_ref, lam_ref)
        a_scr[n] = a
        b_scr[n] = b
    xe_scr[0:8, :] = xe_scr[tc:tc + 8, :]

    row = lax.broadcasted_iota(jnp.int32, (sub, RNN_BLOCK), 0)

    def step(g, carry):
        rows = pl.ds(pl.multiple_of(g * sub, sub), sub)
        new = []
        for n in range(N_RNN_BLOCKS):
            a = a_scr[n, rows, :]
            b = b_scr[n, rows, :]
            for d in (1, 2, 4):
                keep = row >= d
                b = a * jnp.where(keep, pltpu.roll(b, d, 0), 0.0) + b
                a = a * jnp.where(keep, pltpu.roll(a, d, 0), 1.0)
            h = b + a * carry[n]
            b_scr[n, rows, :] = h
            new.append(jnp.broadcast_to(h[sub - 1:sub, :], (sub, RNN_BLOCK)))
        return tuple(new)

    init = tuple(hcar_scr[:, n * RNN_BLOCK:(n + 1) * RNN_BLOCK] for n in range(N_RNN_BLOCKS))
    h_in = lax.fori_loop(0, tc // sub, step, init)

    for n in range(N_RNN_BLOCKS):
        cs = slice(n * RNN_BLOCK, (n + 1) * RNN_BLOCK)
        aout_ref[:, cs] = (b_scr[n] * _gelu_tanh(yg_ref[:, cs].astype(F32))).astype(BF16)
        hcar_scr[:, cs] = h_in[n]
        hlast_ref[:, cs] = h_in[n][0:1, :]


def _rglru_prompt(z, batch, seq, tc, cw, cb, wax, ba, bx, lam):
    nc = seq // tc
    vec = lambda r: pl.BlockSpec((r, D_RNN), lambda b, c: (0, 0))
    return pl.pallas_call(
        functools.partial(_rglru_kernel, tc=tc),
        grid=(batch, nc),
        in_specs=[
            pl.BlockSpec((tc, D_RNN), lambda b, c: (b * nc + c, COL_XR // D_RNN)),
            pl.BlockSpec((tc, D_RNN), lambda b, c: (b * nc + c, COL_YG // D_RNN)),
            vec(CONV_WIDTH), vec(1),
            pl.BlockSpec((N_RNN_BLOCKS, RNN_BLOCK, 2 * RNN_BLOCK), lambda b, c: (0, 0, 0)),
            vec(1), vec(1), vec(1),
        ],
        out_specs=[
            pl.BlockSpec((tc, D_RNN), lambda b, c: (b * nc + c, 0)),
            pl.BlockSpec((None, 1, D_RNN), lambda b, c: (b, 0, 0)),
        ],
        out_shape=[jax.ShapeDtypeStruct((batch * seq, D_RNN), BF16),
                   jax.ShapeDtypeStruct((batch, 1, D_RNN), F32)],
        scratch_shapes=[pltpu.VMEM((tc + 8, D_RNN), F32)]
        + [pltpu.VMEM((N_RNN_BLOCKS, tc, RNN_BLOCK), F32)] * 2
        + [pltpu.VMEM((8, D_RNN), F32)],
        compiler_params=_cparams(2),
        name="rglru",
    )(z, z, cw, cb, wax, ba, bx, lam)


def _rglru_step_kernel(xr_ref, yg_ref, c0_ref, c1_ref, c2_ref, h0_ref, cw_ref, cb_ref, wax_ref, ba_ref,
                       bx_ref, lam_ref, aout_ref, hout_ref):
    taps = (c0_ref, c1_ref, c2_ref)
    for n in range(N_RNN_BLOCKS):
        cs = slice(n * RNN_BLOCK, (n + 1) * RNN_BLOCK)
        xc = cb_ref[:, cs]
        for j in range(CONV_WIDTH - 1):
            xc = xc + taps[j][:, cs] * cw_ref[j:j + 1, cs]
        xc = xc + xr_ref[:, cs].astype(F32) * cw_ref[CONV_WIDTH - 1:CONV_WIDTH, cs]
        a, b = _lru_gates_block(xc, n, wax_ref, ba_ref, bx_ref, lam_ref)
        h = a * h0_ref[:, cs] + b
        hout_ref[:, cs] = h
        aout_ref[:, cs] = (h * _gelu_tanh(yg_ref[:, cs].astype(F32))).astype(BF16)


def _rglru_step(z, conv_taps, h0, cw, cb, wax, ba, bx, lam):
    n = z.shape[0]
    full = lambda shape: pl.BlockSpec(shape, lambda i: (0,) * len(shape))
    return pl.pallas_call(
        _rglru_step_kernel,
        grid=(1,),
        in_specs=[
            pl.BlockSpec((n, D_RNN), lambda i: (0, COL_XR // D_RNN)),
            pl.BlockSpec((n, D_RNN), lambda i: (0, COL_YG // D_RNN)),
            full((n, D_RNN)), full((n, D_RNN)), full((n, D_RNN)), full((n, D_RNN)),
            full((CONV_WIDTH, D_RNN)), full((1, D_RNN)),
            full((N_RNN_BLOCKS, RNN_BLOCK, 2 * RNN_BLOCK)),
            full((1, D_RNN)), full((1, D_RNN)), full((1, D_RNN)),
        ],
        out_specs=[full((n, D_RNN)), full((n, D_RNN))],
        out_shape=[jax.ShapeDtypeStruct((n, D_RNN), BF16), jax.ShapeDtypeStruct((n, D_RNN), F32)],
        compiler_params=_cparams(1),
        name="rglru_step",
    )(z, z, *conv_taps, h0, cw, cb, wax, ba, bx, lam)


def _t5_causal_buckets(dist):
    max_exact = N_BUCKETS // 2
    d = np.maximum(dist, 1).astype(np.float32)
    large = max_exact + (np.log(d / max_exact) / np.log(MAX_DISTANCE / max_exact)
                         * (N_BUCKETS - max_exact)).astype(np.int32)
    large = np.minimum(large, N_BUCKETS - 1)
    return np.where(dist < max_exact, dist, large).astype(np.int32)


def _group_bias(rel_bias, gi, window, dilation):
    buckets = _t5_causal_buckets(np.arange(window // dilation + 1) * dilation)
    heads = slice(gi * HEADS_PER_GROUP, (gi + 1) * HEADS_PER_GROUP)
    return rel_bias[buckets][:, heads].T.astype(F32)


def _band_bias(bias):
    nh, nk = bias.shape
    assert nk == QUERY_BLOCK + 1
    return jnp.concatenate([bias[:, ::-1], jnp.full((nh, BAND_LANES - nk), MASK_VALUE, F32)], axis=1)[:, None, :]


def _attn_kernel(q_ref, k_ref, v_ref, bias_ref, o_ref, lse_ref, q_scr, kc_scr, vc_scr, o_scr, l_scr, *, d, sup):
    sb = pl.program_id(2)
    win = QUERY_BLOCK * d

    @pl.when(sb == 0)
    def _():
        kc_scr[0:win, :] = jnp.zeros((win, HEAD_DIM), F32)
        vc_scr[0:win, :] = jnp.zeros((win, HEAD_DIM), F32)

    q_scr[...] = q_ref[...].astype(F32)
    kc_scr[win:win + sup, :] = k_ref[...].astype(F32)
    vc_scr[win:win + sup, :] = v_ref[...].astype(F32)
    bias = pltpu.roll(jnp.broadcast_to(bias_ref[...], (QUERY_BLOCK, BAND_LANES)), 0, 1, stride=1,
                      stride_axis=0)[:, :2 * QUERY_BLOCK]
    col = lax.broadcasted_iota(jnp.int32, (QUERY_BLOCK, 2 * QUERY_BLOCK), 1)

    def rows(start, size):
        return pl.ds(start, size) if d == 1 else pl.ds(start, size, stride=d)

    for r in range(d):
        for qb in range(sup // win):
            start = r + win * qb
            qu = q_scr[rows(start, QUERY_BLOCK), :].astype(BF16)
            ku = kc_scr[rows(start, 2 * QUERY_BLOCK), :].astype(BF16)
            vu = vc_scr[rows(start, 2 * QUERY_BLOCK), :].astype(BF16)
            s = lax.dot_general(qu, ku, (((1,), (1,)), ((), ())), preferred_element_type=F32) * ATTN_SCALE + bias
            if qb == 0:
                s = jnp.where(jnp.logical_or(sb > 0, col >= QUERY_BLOCK), s, MASK_VALUE)
            m = jnp.max(s, axis=-1, keepdims=True)
            e = jnp.exp(s - m)
            den = jnp.sum(e, axis=-1, keepdims=True)
            o_scr[rows(start, QUERY_BLOCK), :] = jnp.dot((e / den).astype(BF16), vu, preferred_element_type=F32)
            l_scr[rows(start, QUERY_BLOCK), :] = jnp.broadcast_to(m + jnp.log(den), (QUERY_BLOCK, HEAD_DIM))

    o_ref[...] = o_scr[...].astype(BF16)
    lse_ref[...] = l_scr[...]
    kc_scr[0:win, :] = kc_scr[sup:sup + win, :]
    vc_scr[0:win, :] = vc_scr[sup:sup + win, :]


def _attn_prompt(z, band, gi, batch, seq):
    d = GROUPS[gi][1]
    sup = ATTN_SUPER
    win = QUERY_BLOCK * d
    nsb = seq // sup
    qb0 = COL_Q // HEAD_DIM + gi * HEADS_PER_GROUP
    kb0 = COL_K // HEAD_DIM + gi * HEADS_PER_GROUP
    vb0 = COL_V // HEAD_DIM + gi * HEADS_PER_GROUP
    blk = lambda c0: pl.BlockSpec((sup, HEAD_DIM), lambda b, h, s: (b * nsb + s, c0 + h))
    out = pl.BlockSpec((sup, HEAD_DIM), lambda b, h, s: (b * nsb + s, h))
    return pl.pallas_call(
        functools.partial(_attn_kernel, d=d, sup=sup),
        grid=(batch, HEADS_PER_GROUP, nsb),
        in_specs=[blk(qb0), blk(kb0), blk(vb0),
                  pl.BlockSpec((None, 1, BAND_LANES), lambda b, h, s: (h, 0, 0))],
        out_specs=[out, out],
        out_shape=[jax.ShapeDtypeStruct((batch * seq, D_ATTN_OUT), BF16),
                   jax.ShapeDtypeStruct((batch * seq, D_ATTN_OUT), F32)],
        scratch_shapes=[pltpu.VMEM((sup, HEAD_DIM), F32), pltpu.VMEM((win + sup, HEAD_DIM), F32),
                        pltpu.VMEM((win + sup, HEAD_DIM), F32), pltpu.VMEM((sup, HEAD_DIM), F32),
                        pltpu.VMEM((sup, HEAD_DIM), F32)],
        compiler_params=_cparams(3),
        name=f"attn_g{gi}",
    )(z, z, z, band)


def _attn_step_kernel(q_ref, kn_ref, vn_ref, c0_ref, c1_ref, c2_ref, bc_ref, bn_ref, o_ref, lse_ref, *, bb):
    caches = (c0_ref, c1_ref, c2_ref)
    for b in range(bb):
        for g in range(len(GROUPS)):
            hs = slice(g * HEADS_PER_GROUP, (g + 1) * HEADS_PER_GROUP)
            qg = _round_bf16(q_ref[b, hs, :])
            kc = _round_bf16(caches[g][b, :, 0])
            vc = _round_bf16(caches[g][b, :, 1])
            kn = _round_bf16(kn_ref[b, hs, :])
            vn = _round_bf16(vn_ref[b, hs, :])
            sc = jnp.sum(kc * qg[None], axis=-1, keepdims=True) * ATTN_SCALE + bc_ref[g]
            sn = jnp.sum(kn * qg, axis=-1, keepdims=True) * ATTN_SCALE + bn_ref[g]
            m = jnp.maximum(jnp.max(sc, axis=0), sn)
            ec = jnp.exp(sc - m[None])
            en = jnp.exp(sn - m)
            den = jnp.sum(ec, axis=0) + en
            o = jnp.sum(_round_bf16(ec / den[None]) * vc, axis=0) + _round_bf16(en / den) * vn
            o_ref[g, b] = o
            lse_ref[g, b] = jnp.broadcast_to(m + jnp.log(den), (HEADS_PER_GROUP, HEAD_DIM))


def _attn_step(q, kn, vn, caches, bias_c, bias_n, bb):
    n = q.shape[0]
    ng = len(GROUPS)
    tok = pl.BlockSpec((bb, N_ATTN_HEADS, HEAD_DIM), lambda i: (i, 0, 0))
    cache = pl.BlockSpec((bb, QUERY_BLOCK, None, 2, HEADS_PER_GROUP, HEAD_DIM), lambda i: (i, 0, 0, 0, 0, 0))
    out = pl.BlockSpec((ng, bb, HEADS_PER_GROUP, HEAD_DIM), lambda i: (0, i, 0, 0))
    return pl.pallas_call(
        functools.partial(_attn_step_kernel, bb=bb),
        grid=(n // bb,),
        in_specs=[tok, tok, tok, cache, cache, cache,
                  pl.BlockSpec((ng, QUERY_BLOCK, HEADS_PER_GROUP, 1), lambda i: (0, 0, 0, 0)),
                  pl.BlockSpec((ng, HEADS_PER_GROUP, 1), lambda i: (0, 0, 0))],
        out_specs=[out, out],
        out_shape=[jax.ShapeDtypeStruct((ng, n, HEADS_PER_GROUP, HEAD_DIM), F32)] * 2,
        compiler_params=_cparams(1),
        name="attn_step",
    )(q, kn, vn, *caches, bias_c, bias_n)


def _mix_kernel(a_ref, o0_ref, o1_ref, o2_ref, l0_ref, l1_ref, l2_ref, ga_ref, gb_ref, x_ref,
                wr_ref, wa_ref, wo_ref, gf_ref, wrt_ref, brt_ref, cnt0_ref,
                x1_ref, xn_ref, gate_ref, dest_ref, cnt_ref, cnt_scr, *, cap, sub):

    @pl.when(pl.program_id(0) == 0)
    def _():
        cnt_scr[...] = cnt0_ref[...].astype(F32)

    for s in range(x_ref.shape[0] // sub):
        _mix_rows(slice(s * sub, (s + 1) * sub), a_ref, o0_ref, o1_ref, o2_ref, l0_ref, l1_ref, l2_ref, ga_ref,
                  gb_ref, x_ref, wr_ref, wa_ref, wo_ref, gf_ref, wrt_ref, brt_ref,
                  x1_ref, xn_ref, gate_ref, dest_ref, cnt_scr, cap)
    cnt_ref[...] = cnt_scr[...].astype(jnp.int32)


def _mix_rows(rows, a_ref, o0_ref, o1_ref, o2_ref, l0_ref, l1_ref, l2_ref, ga_ref, gb_ref, x_ref,
              wr_ref, wa_ref, wo_ref, gf_ref, wrt_ref, brt_ref, x1_ref, xn_ref, gate_ref, dest_ref, cnt_scr, cap):
    tm = rows.stop - rows.start
    l0, l1, l2 = l0_ref[rows, :], l1_ref[rows, :], l2_ref[rows, :]
    mx = jnp.maximum(jnp.maximum(l0, l1), l2)
    e0, e1, e2 = jnp.exp(l0 - mx), jnp.exp(l1 - mx), jnp.exp(l2 - mx)
    den = e0 + e1 + e2
    b_out = (_round_bf16(e0 / den) * _round_bf16(o0_ref[rows, :]) + _round_bf16(e1 / den) * _round_bf16(o1_ref[rows, :])
             + _round_bf16(e2 / den) * _round_bf16(o2_ref[rows, :]))
    rnn = jnp.dot(a_ref[rows, :], wr_ref[...], preferred_element_type=F32)
    att = jnp.dot(b_out.astype(BF16), wa_ref[...], preferred_element_type=F32)
    merged = (jax.nn.sigmoid(ga_ref[rows, :].astype(F32)) * rnn
              + jax.nn.sigmoid(gb_ref[rows, :].astype(F32)) * att)
    x1 = x_ref[rows, :] + jnp.dot(merged.astype(BF16), wo_ref[...], preferred_element_type=F32)
    x1_ref[rows, :] = x1
    xn = _rms(x1, gf_ref[...])
    _store_row_tiles(xn_ref.at[pl.ds(rows.start * ROW_TILE, tm * ROW_TILE)], xn)

    logits = jnp.dot(xn.astype(BF16), wrt_ref[...], preferred_element_type=F32) + brt_ref[...]
    lane = lax.broadcasted_iota(jnp.int32, (tm, LANES), 1)
    cur = jnp.where(lane < N_EXPERTS, logits, -jnp.inf)
    vals, idxs = [], []
    for _ in range(TOP_K):
        mk = jnp.max(cur, axis=-1, keepdims=True)
        ik = jnp.min(jnp.where(cur == mk, lane, LANES), axis=-1, keepdims=True)
        vals.append(mk)
        idxs.append(ik)
        cur = jnp.where(lane == ik, -jnp.inf, cur)
    es = [jnp.exp(v - vals[0]) for v in vals]
    den = es[0] + es[1] + es[2] + es[3]
    chosen = jnp.zeros((tm, LANES), F32)
    for k in range(TOP_K):
        chosen = chosen + jnp.where(lane == idxs[k], 1.0, 0.0)
    tri = jnp.where(lax.broadcasted_iota(jnp.int32, (tm, tm), 1) <= lax.broadcasted_iota(jnp.int32, (tm, tm), 0),
                    1.0, 0.0).astype(BF16)
    incl = jnp.dot(tri, chosen.astype(BF16), preferred_element_type=F32)
    rank = cnt_scr[...] + incl - chosen
    gate_out = jnp.zeros((tm, LANES), F32)
    dest_out = jnp.zeros((tm, LANES), jnp.int32)
    for k in range(TOP_K):
        rank_k = jnp.sum(jnp.where(lane == idxs[k], rank, 0.0), axis=-1, keepdims=True).astype(jnp.int32)
        gate_out = jnp.where(lane == k, es[k] / den, gate_out)
        dest_out = jnp.where(lane == k, idxs[k] * cap + rank_k, dest_out)
    gate_ref[rows, :] = gate_out
    if tm % LANES:
        dest_out = jnp.concatenate([dest_out, jnp.zeros((LANES - tm % LANES, LANES), jnp.int32)], axis=0)
    dest_ref[:, rows] = dest_out.T[0:DEST_ROWS, 0:tm]
    cnt_scr[...] = cnt_scr[...] + incl[tm - 1:tm, :]


def _mix(a_out, os_, ls_, z, x2d, wr, wa, wo, gf, wrt, brt, cnt0, tm, cap):
    n = x2d.shape[0]
    row = lambda w, c=0: pl.BlockSpec((tm, w), lambda i: (i, c))
    full = lambda shape: pl.BlockSpec(shape, lambda i: (0,) * len(shape))
    return pl.pallas_call(
        functools.partial(_mix_kernel, cap=cap, sub=min(tm, MIX_SUB)),
        grid=(n // tm,),
        in_specs=[row(D_RNN), row(D_ATTN_OUT), row(D_ATTN_OUT), row(D_ATTN_OUT),
                  row(D_ATTN_OUT), row(D_ATTN_OUT), row(D_ATTN_OUT),
                  row(D_MODEL, COL_GA // D_MODEL), row(D_MODEL, COL_GB // D_MODEL), row(D_MODEL),
                  full((D_RNN, D_MODEL)), full((D_ATTN_OUT, D_MODEL)), full((D_MODEL, D_MODEL)),
                  full((1, D_MODEL)), full((D_MODEL, LANES)), full((1, LANES)), full((1, LANES))],
        out_specs=[row(D_MODEL), pl.BlockSpec((tm * ROW_TILE, LANES), lambda i: (i, 0)), row(LANES),
                   pl.BlockSpec((None, DEST_ROWS, tm), lambda i: (i, 0, 0)), full((1, LANES))],
        out_shape=[jax.ShapeDtypeStruct((n, D_MODEL), F32), jax.ShapeDtypeStruct((n * ROW_TILE, LANES), F32),
                   jax.ShapeDtypeStruct((n, LANES), F32), jax.ShapeDtypeStruct((n // tm, DEST_ROWS, tm), jnp.int32),
                   jax.ShapeDtypeStruct((1, LANES), jnp.int32)],
        scratch_shapes=[pltpu.VMEM((1, LANES), F32)],
        compiler_params=_cparams(1),
        name="mix",
    )(a_out, *os_, *ls_, z, z, x2d, wr, wa, wo, gf, wrt, brt, cnt0)


DMA_UNROLL = 8
DEST_ROWS = 8
MIX_TM = 512
MIX_SUB = 256


def _start_rows(tm, copies):
    def body(r, carry):
        for k, c in enumerate(copies(r)):
            c.start(priority=k % 2)
        return carry

    lax.fori_loop(0, tm, body, 0, unroll=DMA_UNROLL)


def _wait_rows(tm, copies):
    first = copies(0)

    def body(r, carry):
        for c in first:
            c.wait()
        return carry

    lax.fori_loop(0, tm, body, 0, unroll=DMA_UNROLL)


def _dispatch_kernel(dprev_ref, dcur_ref, xn_hbm, *rest, tm):
    xb_ref, src, load_sem, row_sem = rest[-4:]
    i = pl.program_id(0)
    n = pl.num_programs(0)
    slot = i % 2

    def load(tile, s):
        rows = pl.ds(pl.multiple_of(tile * (tm * ROW_TILE), ROW_TILE), tm * ROW_TILE)
        return pltpu.make_async_copy(xn_hbm.at[rows], src.at[s], load_sem.at[s])

    def scatter(dest_ref, s):
        return lambda r: [_row_tile_copy(src.at[s], r, xb_ref, dest_ref[k * tm + r], row_sem.at[s])
                          for k in range(TOP_K)]

    @pl.when(i == 0)
    def _():
        load(0, 0).start()

    @pl.when(i > 0)
    def _():
        _wait_rows(tm, scatter(dprev_ref, 1 - slot))

    @pl.when(i + 1 < n)
    def _():
        load(i + 1, 1 - slot).start()

    load(i, slot).wait()
    _start_rows(tm, scatter(dcur_ref, slot))

    @pl.when(i == n - 1)
    def _():
        _wait_rows(tm, scatter(dcur_ref, slot))


def _dispatch(dest_flat, xn, xb, tm, rows):
    n = xn.shape[0] // ROW_TILE
    any_spec = pl.BlockSpec(memory_space=pl.ANY)
    dest_spec = lambda off: pl.BlockSpec((DEST_ROWS * tm,), lambda i: (jnp.maximum(i + off, 0),),
                                         memory_space=pltpu.SMEM)
    in_specs = [dest_spec(-1), dest_spec(0), any_spec]
    args = [dest_flat, dest_flat, xn]
    aliases = {}
    if xb is not None:
        in_specs.append(any_spec)
        args.append(xb)
        aliases = {3: 0}
    return pl.pallas_call(
        functools.partial(_dispatch_kernel, tm=tm),
        grid=(n // tm,),
        in_specs=in_specs,
        out_specs=any_spec,
        out_shape=jax.ShapeDtypeStruct((rows * ROW_TILE, LANES), F32),
        scratch_shapes=[pltpu.VMEM((2, tm * ROW_TILE, LANES), F32), pltpu.SemaphoreType.DMA((2,)),
                        pltpu.SemaphoreType.DMA((2,))],
        input_output_aliases=aliases,
        compiler_params=_cparams(1),
        name="dispatch",
    )(*args)


def _moe_kernel(be_ref, br_ref, nx_ref, sl_ref, nu_ref, xb_ref, wgu_hbm, bgu_ref, wd_hbm, bd_ref, y_ref,
                wgu_f32, wd_f32, wgu_bf, wd_bf, sems):
    i = pl.program_id(0)
    e = be_ref[i]
    slot = sl_ref[i]
    changed = jnp.logical_or(i == 0, e != be_ref[jnp.maximum(i - 1, 0)])

    def fetch(expert, s):
        return (pltpu.make_async_copy(wgu_hbm.at[expert], wgu_f32.at[s], sems.at[0, s]),
                pltpu.make_async_copy(wd_hbm.at[expert], wd_f32.at[s], sems.at[1, s]))

    @pl.when(i == 0)
    def _():
        for c in fetch(e, slot):
            c.start()

    @pl.when(changed)
    def _():
        @pl.when(nx_ref[i] >= 0)
        def _():
            for c in fetch(nx_ref[i], 1 - slot):
                c.start(priority=1)

        for c in fetch(e, slot):
            c.wait()
        wgu_bf[...] = wgu_f32[slot].astype(BF16)
        wd_bf[...] = wd_f32[slot].astype(BF16)

    @pl.when(i < nu_ref[0])
    def _():
        xb = _load_row_tiles(xb_ref, EXPERT_ROWS).astype(BF16)
        gu = jnp.dot(xb, wgu_bf[...], preferred_element_type=F32) + bgu_ref[...]
        g = jnp.minimum(gu[:, :D_EXPERT], SWIGLU_LIMIT)
        u = jnp.clip(gu[:, D_EXPERT:], -SWIGLU_LIMIT, SWIGLU_LIMIT)
        act = (u + 1.0) * (g * jax.nn.sigmoid(SWIGLU_ALPHA * g))
        _store_row_tiles(y_ref, jnp.dot(act.astype(BF16), wd_bf[...], preferred_element_type=F32) + bd_ref[...])


def _moe(sched, xb, wgu, bgu, wd, bd):
    eb = EXPERT_ROWS
    block_e = sched[0]
    any_spec = pl.BlockSpec(memory_space=pl.ANY)
    grid_spec = pltpu.PrefetchScalarGridSpec(
        num_scalar_prefetch=len(sched),
        grid=(block_e.shape[0],),
        in_specs=[
            pl.BlockSpec((eb * ROW_TILE, LANES), lambda i, be, br, *_: (br[i], 0)),
            any_spec,
            pl.BlockSpec((None, 1, 2 * D_EXPERT), lambda i, be, *_: (be[i], 0, 0)),
            any_spec,
            pl.BlockSpec((None, 1, D_MODEL), lambda i, be, *_: (be[i], 0, 0)),
        ],
        out_specs=pl.BlockSpec((eb * ROW_TILE, LANES), lambda i, be, br, *_: (br[i], 0)),
        scratch_shapes=[pltpu.VMEM((2, D_MODEL, 2 * D_EXPERT), F32), pltpu.VMEM((2, D_EXPERT, D_MODEL), F32),
                        pltpu.VMEM((D_MODEL, 2 * D_EXPERT), BF16), pltpu.VMEM((D_EXPERT, D_MODEL), BF16),
                        pltpu.SemaphoreType.DMA((2, 2))],
    )
    return pl.pallas_call(
        _moe_kernel,
        grid_spec=grid_spec,
        out_shape=jax.ShapeDtypeStruct(xb.shape, F32),
        compiler_params=_cparams(1),
        name="moe",
    )(*sched, xb, wgu, bgu, wd, bd)


def _expert_blocks(counts, cap, n_assign):
    eb = EXPERT_ROWS
    n_grid = -(-n_assign // eb) + N_EXPERTS
    nblk = (counts + eb - 1) // eb
    blk_end = jnp.cumsum(nblk)
    n_used = blk_end[-1:].astype(jnp.int32)
    i = jnp.minimum(jnp.arange(n_grid, dtype=jnp.int32), n_used[0] - 1)
    block_e = jnp.sum(blk_end[None, :] <= i[:, None], axis=1).astype(jnp.int32)
    block_row = block_e * (cap // eb) + i - (blk_end - nblk)[block_e]
    ids = jnp.arange(N_EXPERTS, dtype=jnp.int32)
    used = nblk > 0
    later_used = jnp.logical_and(used[None, :], ids[None, :] > ids[:, None])
    next_e = jnp.min(jnp.where(later_used, ids[None, :], N_EXPERTS), axis=1)
    next_e = jnp.where(next_e < N_EXPERTS, next_e, -1).astype(jnp.int32)
    slot_e = ((jnp.cumsum(used.astype(jnp.int32)) - 1) % 2).astype(jnp.int32)
    return (block_e, block_row.astype(jnp.int32), next_e[block_e], slot_e[block_e], n_used)


def _ple_kernel(dcur_ref, dnext_ref, x1_ref, gate_ref, p_ref, gp_ref, wg_ref, wp_ref, yb_ref, y_ref, ybuf, sems):
    tm = x1_ref.shape[0]
    i = pl.program_id(0)
    slot = i % 2

    def gather(dest_ref, s):
        return lambda r: [_row_tile_copy(yb_ref, dest_ref[k * tm + r], ybuf.at[s, k], r, sems.at[s])
                          for k in range(TOP_K)]

    @pl.when(i == 0)
    def _():
        _start_rows(tm, gather(dcur_ref, 0))

    @pl.when(i + 1 < pl.num_programs(0))
    def _():
        _start_rows(tm, gather(dnext_ref, 1 - slot))

    _wait_rows(tm, gather(dcur_ref, slot))

    gates = gate_ref[...]
    x2 = x1_ref[...]
    for k in range(TOP_K):
        x2 = x2 + _load_row_tiles(ybuf.at[slot, k], tm) * gates[:, k:k + 1]
    gate = jax.nn.sigmoid(jnp.dot(_rms(x2, gp_ref[...]).astype(BF16), wg_ref[...], preferred_element_type=F32))
    proj = jnp.dot(p_ref[...].astype(BF16), wp_ref[...], preferred_element_type=F32)
    y_ref[...] = x2 + gate * proj


def _ple(dest_flat, x1, gates, p2d, gp, wg, wp, yb, tm):
    n = x1.shape[0]
    row = lambda w: pl.BlockSpec((tm, w), lambda i: (i, 0))
    full = lambda shape: pl.BlockSpec(shape, lambda i: (0,) * len(shape))
    last = n // tm - 1
    dest_spec = lambda off: pl.BlockSpec((DEST_ROWS * tm,), lambda i: (jnp.minimum(i + off, last),),
                                         memory_space=pltpu.SMEM)
    return pl.pallas_call(
        _ple_kernel,
        grid=(n // tm,),
        in_specs=[dest_spec(0), dest_spec(1),
                  row(D_MODEL), row(LANES), row(D_PLE), full((1, D_MODEL)),
                  full((D_MODEL, D_MODEL)), full((D_PLE, D_MODEL)), pl.BlockSpec(memory_space=pl.ANY)],
        out_specs=row(D_MODEL),
        out_shape=jax.ShapeDtypeStruct((n, D_MODEL), F32),
        scratch_shapes=[pltpu.VMEM((2, TOP_K, tm * ROW_TILE, LANES), F32), pltpu.SemaphoreType.DMA((2,))],
        compiler_params=_cparams(1),
        name="ple",
    )(dest_flat, dest_flat, x1, gates, p2d, gp, wg, wp, yb)


def kernel(x_prompt, x_sample, state_conv, state_h, cache_kv_w128, cache_kv_w512, cache_kv_w2048, p_prompt,
           p_sample, norm_mix_g, w_in, conv_w, conv_b, w_rg_a, b_rg_a, w_rg_x, b_rg_x, lru_lambda, q_norm_g,
           k_norm_g, rel_bias, w_br_rnn, w_br_attn, w_out, norm_ffn_g, w_router, b_router, w_gate_up, b_gate_up,
           w_down, b_down, norm_ple_g, w_ple_gate, w_ple_proj):
    bp, seq, _ = x_prompt.shape
    bs = x_sample.shape[0]
    n_p = bp * seq
    li = 0
    row = lambda v: v.reshape(1, -1).astype(F32)

    w = w_in[li]
    o_q, o_ga = 2 * D_RNN, 2 * D_RNN + 3 * D_QKV
    w_perm = jnp.concatenate([w[:, :o_q], w[:, o_ga:], w[:, o_q:o_ga]], axis=1).astype(BF16)
    colgain = jnp.concatenate([jnp.ones((COL_Q,), F32), jnp.tile(q_norm_g[li], N_ATTN_HEADS),
                               jnp.tile(k_norm_g[li], N_ATTN_HEADS), jnp.ones((D_QKV,), F32)]).reshape(1, D_IN)
    wax = jnp.concatenate([w_rg_a[li], w_rg_x[li]], axis=-1).astype(BF16)
    lru = (conv_w[li].astype(F32), row(conv_b[li]), wax, row(b_rg_a[li]), row(b_rg_x[li]), row(lru_lambda[li]))
    wr, wa, wo = w_br_rnn[li].astype(BF16), w_br_attn[li].astype(BF16), w_out[li].astype(BF16)
    wrt = jnp.pad(w_router[li].astype(BF16), ((0, 0), (0, LANES - N_EXPERTS)))
    brt = jnp.pad(row(b_router[li]), ((0, 0), (0, LANES - N_EXPERTS)))
    gf = row(norm_ffn_g[li])
    biases = [_group_bias(rel_bias, gi, wnd, dil) for gi, (wnd, dil) in enumerate(GROUPS)]

    xp = x_prompt.reshape(n_p, D_MODEL)
    zp = _inproj(xp, row(norm_mix_g[li]), w_perm, colgain, tm=1024, out_dtype=BF16)
    a_p, h_p = _rglru_prompt(zp, bp, seq, 512, *lru)
    os_p, ls_p = [], []
    for gi in range(len(GROUPS)):
        o, l = _attn_prompt(zp, _band_bias(biases[gi]), gi, bp, seq)
        os_p.append(o)
        ls_p.append(l)
    cap = -(-(n_p + bs) // EXPERT_ROWS) * EXPERT_ROWS
    x1_p, xn_p, gate_p, dest_p, cnt_p = _mix(a_p, os_p, ls_p, zp, xp, wr, wa, wo, gf, wrt, brt,
                                              jnp.zeros((1, LANES), jnp.int32), tm=MIX_TM, cap=cap)

    xs = x_sample.reshape(bs, D_MODEL)
    zs = _inproj(xs, row(norm_mix_g[li]), w_perm, colgain, tm=bs, out_dtype=F32)
    sc = state_conv[li]
    a_s, h_s = _rglru_step(zs, [sc[:, j] for j in range(CONV_WIDTH - 1)], state_h[li], *lru)
    heads = lambda c0: zs[:, c0:c0 + D_QKV].astype(F32).reshape(bs, N_ATTN_HEADS, HEAD_DIM)
    caches = [c[li].reshape(bs, QUERY_BLOCK, dil, 2, HEADS_PER_GROUP, HEAD_DIM)
              for c, (_, dil) in zip((cache_kv_w128, cache_kv_w512, cache_kv_w2048), GROUPS)]
    bias_c = jnp.stack([b[:, :0:-1].T for b in biases])[..., None]
    bias_n = jnp.stack([b[:, 0] for b in biases])[..., None]
    o_s, l_s = _attn_step(heads(COL_Q), heads(COL_K), heads(COL_V), caches, bias_c, bias_n, bb=4)
    os_s = [o_s[g].reshape(bs, D_ATTN_OUT) for g in range(len(GROUPS))]
    ls_s = [l_s[g].reshape(bs, D_ATTN_OUT) for g in range(len(GROUPS))]
    x1_s, xn_s, gate_s, dest_s, cnt_s = _mix(a_s, os_s, ls_s, zs, xs, wr, wa, wo, gf, wrt, brt, cnt_p, tm=bs, cap=cap)

    dest_p, dest_s = dest_p.reshape(-1), dest_s.reshape(-1)
    xb = _dispatch(dest_p, xn_p, None, MIX_TM, N_EXPERTS * cap)
    xb = _dispatch(dest_s, xn_s, xb, bs, N_EXPERTS * cap)
    sched = _expert_blocks(cnt_s[0, :N_EXPERTS], cap, (n_p + bs) * TOP_K)
    yb = _moe(sched, xb, w_gate_up[li], b_gate_up[li][:, None, :], w_down[li], b_down[li][:, None, :])

    gp, wg, wp = row(norm_ple_g[li]), w_ple_gate[li].astype(BF16), w_ple_proj[li].astype(BF16)
    y_p = _ple(dest_p, x1_p, gate_p, p_prompt[li].reshape(n_p, D_PLE), gp, wg, wp, yb, tm=MIX_TM)
    y_s = _ple(dest_s, x1_s, gate_s, p_sample[li].reshape(bs, D_PLE), gp, wg, wp, yb, tm=bs)

    z3 = zp.reshape(bp, seq, D_IN)
    zs3 = zs.reshape(bs, 1, D_IN)

    def new_kv(zz, gi, keep):
        t = zz.shape[1]
        c = gi * D_ATTN_OUT
        kk = zz[:, t - keep:, COL_K + c:COL_K + c + D_ATTN_OUT].astype(F32)
        vv = zz[:, t - keep:, COL_V + c:COL_V + c + D_ATTN_OUT].astype(F32)
        shape = (zz.shape[0], keep, HEADS_PER_GROUP, HEAD_DIM)
        return jnp.stack([kk.reshape(shape), vv.reshape(shape)], axis=2)[None]

    conv_p = z3[:, seq - (CONV_WIDTH - 1):, COL_XR:COL_XR + D_RNN].astype(F32)[None]
    conv_s = jnp.concatenate([sc[:, 1:], zs3[:, :, COL_XR:COL_XR + D_RNN].astype(F32)], axis=1)[None]
    outs = [y_p.reshape(bp, seq, D_MODEL), y_s.reshape(bs, 1, D_MODEL), conv_p, conv_s,
            h_p.reshape(1, bp, D_RNN), h_s.reshape(1, bs, D_RNN)]
    for gi, (wnd, _) in enumerate(GROUPS):
        outs.append(new_kv(z3, gi, min(wnd, seq)))
        outs.append(new_kv(zs3, gi, 1))
    return tuple(outs)
```

```python
import functools
import math

import numpy as np
import jax
import jax.numpy as jnp
from jax import lax
from jax.experimental import pallas as pl
from jax.experimental.pallas import tpu as pltpu

F32 = jnp.float32
BF16 = jnp.bfloat16

D_MODEL = 1024
D_RNN = 1024
N_RNN_BLOCKS = 8
RNN_BLOCK = D_RNN // N_RNN_BLOCKS
CONV_WIDTH = 4
LRU_C = 8.0
GROUPS = ((128, 1), (512, 4), (2048, 16))
HEADS_PER_GROUP = 4
HEAD_DIM = 128
N_ATTN_HEADS = HEADS_PER_GROUP * len(GROUPS)
D_QKV = N_ATTN_HEADS * HEAD_DIM
D_ATTN_OUT = HEADS_PER_GROUP * HEAD_DIM
N_BUCKETS = 32
MAX_DISTANCE = 2048
QUERY_BLOCK = 128
D_IN = 2 * D_RNN + 3 * D_QKV + 2 * D_MODEL
N_EXPERTS = 32
TOP_K = 4
D_EXPERT = 1024
SWIGLU_ALPHA = 1.702
SWIGLU_LIMIT = 7.0
D_PLE = 256
EPS = 1e-6
ATTN_SCALE = HEAD_DIM ** -0.5
MASK_VALUE = -1e30

LANES = 128
ROW_TILE = D_MODEL // LANES
COL_XR, COL_YG, COL_GA, COL_GB = 0, D_RNN, 2 * D_RNN, 2 * D_RNN + D_MODEL
COL_Q = 2 * D_RNN + 2 * D_MODEL
COL_K = COL_Q + D_QKV
COL_V = COL_K + D_QKV
INPROJ_TN = D_IN // 4
INPROJ_CHUNKS = ((0, 512), (512, 512), (1024, 512), (1536, 640))
INPROJ_SUB = 1024
ATTN_SUPER = 2048
BAND_LANES = 512
EXPERT_ROWS = 512
VMEM_LIMIT = 56 * 1024 * 1024


def _cparams(n_axes):
    return pltpu.CompilerParams(dimension_semantics=("arbitrary",) * n_axes, vmem_limit_bytes=VMEM_LIMIT)


def _rms(x, g):
    ms = jnp.mean(x * x, axis=-1, keepdims=True)
    return (x * lax.rsqrt(ms + EPS)) * g


def _round_bf16(x):
    return x.astype(BF16).astype(F32)


def _load_row_tiles(ref, n_rows):
    return jnp.concatenate([ref[pl.ds(c, n_rows, stride=ROW_TILE), :] for c in range(ROW_TILE)], axis=1)


def _store_row_tiles(ref, value):
    n_rows = value.shape[0]
    for c in range(ROW_TILE):
        ref[pl.ds(c, n_rows, stride=ROW_TILE), :] = value[:, c * LANES:(c + 1) * LANES]


def _row_tile_copy(src_ref, src_row, dst_ref, dst_row, sem):
    src = src_ref.at[pl.ds(pl.multiple_of(src_row * ROW_TILE, ROW_TILE), ROW_TILE)]
    dst = dst_ref.at[pl.ds(pl.multiple_of(dst_row * ROW_TILE, ROW_TILE), ROW_TILE)]
    return pltpu.make_async_copy(src, dst, sem)


def _inproj_kernel(x_ref, g_ref, w_ref, cg_ref, z_ref, u_scr, *, sub):
    j = pl.program_id(1)
    n_sub = x_ref.shape[0] // sub

    @pl.when(j == 0)
    def _():
        for s in range(n_sub):
            rows = slice(s * sub, (s + 1) * sub)
            u_scr[rows, :] = _rms(x_ref[rows, :], g_ref[...]).astype(BF16)

    for s in range(n_sub):
        rows = slice(s * sub, (s + 1) * sub)
        for c0, width in INPROJ_CHUNKS:
            acc = jnp.dot(u_scr[rows, :], w_ref[:, c0:c0 + width], preferred_element_type=F32)
            for h in range(width // HEAD_DIM):
                cs = slice(c0 + h * HEAD_DIM, c0 + (h + 1) * HEAD_DIM)
                col = j * INPROJ_TN + cs.start
                is_qk = jnp.logical_and(col >= COL_Q, col < COL_V)
                a = acc[:, h * HEAD_DIM:(h + 1) * HEAD_DIM]
                z_ref[rows, cs] = jnp.where(is_qk, _rms(a, cg_ref[:, cs]), a).astype(z_ref.dtype)


def _inproj(x2d, g, w_perm, colgain, tm, out_dtype):
    n = x2d.shape[0]
    return pl.pallas_call(
        functools.partial(_inproj_kernel, sub=min(tm, INPROJ_SUB)),
        grid=(n // tm, D_IN // INPROJ_TN),
        in_specs=[
            pl.BlockSpec((tm, D_MODEL), lambda i, j: (i, 0)),
            pl.BlockSpec((1, D_MODEL), lambda i, j: (0, 0)),
            pl.BlockSpec((D_MODEL, INPROJ_TN), lambda i, j: (0, j)),
            pl.BlockSpec((1, INPROJ_TN), lambda i, j: (0, j)),
        ],
        out_specs=pl.BlockSpec((tm, INPROJ_TN), lambda i, j: (i, j)),
        out_shape=jax.ShapeDtypeStruct((n, D_IN), out_dtype),
        scratch_shapes=[pltpu.VMEM((tm, D_MODEL), BF16)],
        compiler_params=_cparams(2),
        name="inproj",
    )(x2d, g, w_perm, colgain)


GELU_C1 = 2.0 * math.sqrt(2.0 / math.pi)
GELU_C2 = GELU_C1 * 0.044715


def _gelu_tanh(x):
    return x / (1.0 + jnp.exp(x * (-GELU_C1 - GELU_C2 * (x * x))))


def _softplus(v):
    return jnp.maximum(v, 0.0) + jnp.log1p(jnp.exp(-jnp.abs(v)))


def _lru_gates_block(xc, n, wax_ref, ba_ref, bx_ref, lam_ref):
    cs = slice(n * RNN_BLOCK, (n + 1) * RNN_BLOCK)
    ri = jnp.dot(xc.astype(BF16), wax_ref[n], preferred_element_type=F32)
    r = jax.nn.sigmoid(ri[:, :RNN_BLOCK] + ba_ref[:, cs])
    i = jax.nn.sigmoid(ri[:, RNN_BLOCK:] + bx_ref[:, cs])
    log_a = (-LRU_C * r) * _softplus(-lam_ref[:, cs])
    a = jnp.exp(log_a)
    v = 1.0 - a * a
    b = jnp.where(v > 0.0, v * lax.rsqrt(v), 0.0) * (i * xc)
    return a, b


def _rglru_kernel(xr_ref, yg_ref, cw_ref, cb_ref, wax_ref, ba_ref, bx_ref, lam_ref,
                  aout_ref, hlast_ref, xe_scr, a_scr, b_scr, hcar_scr, *, tc):
    c = pl.program_id(1)
    sub = 8

    @pl.when(c == 0)
    def _():
        xe_scr[0:8, :] = jnp.zeros((8, D_RNN), F32)
        hcar_scr[...] = jnp.zeros((sub, D_RNN), F32)

    xe_scr[8:8 + tc, :] = xr_ref[...].astype(F32)
    for n in range(N_RNN_BLOCKS):
        cs = slice(n * RNN_BLOCK, (n + 1) * RNN_BLOCK)
        xc = cb_ref[:, cs]
        for j in range(CONV_WIDTH):
            xc = xc + xe_scr[pl.ds(8 - (CONV_WIDTH - 1) + j, tc), cs] * cw_ref[j:j + 1, cs]
        a, b = _lru_gates_block(xc, n, wax_ref, ba_ref, bx_ref, lam_ref)
        a_scr[n] = a
        b_scr[n] = b
    xe_scr[0:8, :] = xe_scr[tc:tc + 8, :]

    row = lax.broadcasted_iota(jnp.int32, (sub, RNN_BLOCK), 0)

    def step(g, carry):
        rows = pl.ds(pl.multiple_of(g * sub, sub), sub)
        new = []
        for n in range(N_RNN_BLOCKS):
            a = a_scr[n, rows, :]
            b = b_scr[n, rows, :]
            for d in (1, 2, 4):
                keep = row >= d
                b = a * jnp.where(keep, pltpu.roll(b, d, 0), 0.0) + b
                a = a * jnp.where(keep, pltpu.roll(a, d, 0), 1.0)
            h = b + a * carry[n]
            b_scr[n, rows, :] = h
            new.append(jnp.broadcast_to(h[sub - 1:sub, :], (sub, RNN_BLOCK)))
        return tuple(new)

    init = tuple(hcar_scr[:, n * RNN_BLOCK:(n + 1) * RNN_BLOCK] for n in range(N_RNN_BLOCKS))
    h_in = lax.fori_loop(0, tc // sub, step, init)

    for n in range(N_RNN_BLOCKS):
        cs = slice(n * RNN_BLOCK, (n + 1) * RNN_BLOCK)
        aout_ref[:, cs] = (b_scr[n] * _gelu_tanh(yg_ref[:, cs].astype(F32))).astype(BF16)
        hcar_scr[:, cs] = h_in[n]
        hlast_ref[:, cs] = h_in[n][0:1, :]


def _rglru_prompt(z, batch, seq, tc, cw, cb, wax, ba, bx, lam):
    nc = seq // tc
    vec = lambda r: pl.BlockSpec((r, D_RNN), lambda b, c: (0, 0))
    return pl.pallas_call(
        functools.partial(_rglru_kernel, tc=tc),
        grid=(batch, nc),
        in_specs=[
            pl.BlockSpec((tc, D_RNN), lambda b, c: (b * nc + c, COL_XR // D_RNN)),
            pl.BlockSpec((tc, D_RNN), lambda b, c: (b * nc + c, COL_YG // D_RNN)),
            vec(CONV_WIDTH), vec(1),
            pl.BlockSpec((N_RNN_BLOCKS, RNN_BLOCK, 2 * RNN_BLOCK), lambda b, c: (0, 0, 0)),
            vec(1), vec(1), vec(1),
        ],
        out_specs=[
            pl.BlockSpec((tc, D_RNN), lambda b, c: (b * nc + c, 0)),
            pl.BlockSpec((None, 1, D_RNN), lambda b, c: (b, 0, 0)),
        ],
        out_shape=[jax.ShapeDtypeStruct((batch * seq, D_RNN), BF16),
                   jax.ShapeDtypeStruct((batch, 1, D_RNN), F32)],
        scratch_shapes=[pltpu.VMEM((tc + 8, D_RNN), F32)]
        + [pltpu.VMEM((N_RNN_BLOCKS, tc, RNN_BLOCK), F32)] * 2
        + [pltpu.VMEM((8, D_RNN), F32)],
        compiler_params=_cparams(2),
        name="rglru",
    )(z, z, cw, cb, wax, ba, bx, lam)


def _rglru_step_kernel(xr_ref, yg_ref, c0_ref, c1_ref, c2_ref, h0_ref, cw_ref, cb_ref, wax_ref, ba_ref,
                       bx_ref, lam_ref, aout_ref, hout_ref):
    taps = (c0_ref, c1_ref, c2_ref)
    for n in range(N_RNN_BLOCKS):
        cs = slice(n * RNN_BLOCK, (n + 1) * RNN_BLOCK)
        xc = cb_ref[:, cs]
        for j in range(CONV_WIDTH - 1):
            xc = xc + taps[j][:, cs] * cw_ref[j:j + 1, cs]
        xc = xc + xr_ref[:, cs].astype(F32) * cw_ref[CONV_WIDTH - 1:CONV_WIDTH, cs]
        a, b = _lru_gates_block(xc, n, wax_ref, ba_ref, bx_ref, lam_ref)
        h = a * h0_ref[:, cs] + b
        hout_ref[:, cs] = h
        aout_ref[:, cs] = (h * _gelu_tanh(yg_ref[:, cs].astype(F32))).astype(BF16)


def _rglru_step(z, conv_taps, h0, cw, cb, wax, ba, bx, lam):
    n = z.shape[0]
    full = lambda shape: pl.BlockSpec(shape, lambda i: (0,) * len(shape))
    return pl.pallas_call(
        _rglru_step_kernel,
        grid=(1,),
        in_specs=[
            pl.BlockSpec((n, D_RNN), lambda i: (0, COL_XR // D_RNN)),
            pl.BlockSpec((n, D_RNN), lambda i: (0, COL_YG // D_RNN)),
            full((n, D_RNN)), full((n, D_RNN)), full((n, D_RNN)), full((n, D_RNN)),
            full((CONV_WIDTH, D_RNN)), full((1, D_RNN)),
            full((N_RNN_BLOCKS, RNN_BLOCK, 2 * RNN_BLOCK)),
            full((1, D_RNN)), full((1, D_RNN)), full((1, D_RNN)),
        ],
        out_specs=[full((n, D_RNN)), full((n, D_RNN))],
        out_shape=[jax.ShapeDtypeStruct((n, D_RNN), BF16), jax.ShapeDtypeStruct((n, D_RNN), F32)],
        compiler_params=_cparams(1),
        name="rglru_step",
    )(z, z, *conv_taps, h0, cw, cb, wax, ba, bx, lam)


def _t5_causal_buckets(dist):
    max_exact = N_BUCKETS // 2
    d = np.maximum(dist, 1).astype(np.float32)
    large = max_exact + (np.log(d / max_exact) / np.log(MAX_DISTANCE / max_exact)
                         * (N_BUCKETS - max_exact)).astype(np.int32)
    large = np.minimum(large, N_BUCKETS - 1)
    return np.where(dist < max_exact, dist, large).astype(np.int32)


def _group_bias(rel_bias, gi, window, dilation):
    buckets = _t5_causal_buckets(np.arange(window // dilation + 1) * dilation)
    heads = slice(gi * HEADS_PER_GROUP, (gi + 1) * HEADS_PER_GROUP)
    return rel_bias[buckets][:, heads].T.astype(F32)


def _band_bias(bias):
    nh, nk = bias.shape
    assert nk == QUERY_BLOCK + 1
    return jnp.concatenate([bias[:, ::-1], jnp.full((nh, BAND_LANES - nk), MASK_VALUE, F32)], axis=1)[:, None, :]


def _attn_kernel(q_ref, k_ref, v_ref, bias_ref, o_ref, lse_ref, q_scr, kc_scr, vc_scr, o_scr, l_scr, *, d, sup):
    sb = pl.program_id(2)
    win = QUERY_BLOCK * d

    @pl.when(sb == 0)
    def _():
        kc_scr[0:win, :] = jnp.zeros((win, HEAD_DIM), F32)
        vc_scr[0:win, :] = jnp.zeros((win, HEAD_DIM), F32)

    q_scr[...] = q_ref[...].astype(F32)
    kc_scr[win:win + sup, :] = k_ref[...].astype(F32)
    vc_scr[win:win + sup, :] = v_ref[...].astype(F32)
    bias = pltpu.roll(jnp.broadcast_to(bias_ref[...], (QUERY_BLOCK, BAND_LANES)), 0, 1, stride=1,
                      stride_axis=0)[:, :2 * QUERY_BLOCK]
    col = lax.broadcasted_iota(jnp.int32, (QUERY_BLOCK, 2 * QUERY_BLOCK), 1)

    def rows(start, size):
        return pl.ds(start, size) if d == 1 else pl.ds(start, size, stride=d)

    for r in range(d):
        for qb in range(sup // win):
            start = r + win * qb
            qu = q_scr[rows(start, QUERY_BLOCK), :].astype(BF16)
            ku = kc_scr[rows(start, 2 * QUERY_BLOCK), :].astype(BF16)
            vu = vc_scr[rows(start, 2 * QUERY_BLOCK), :].astype(BF16)
            s = lax.dot_general(qu, ku, (((1,), (1,)), ((), ())), preferred_element_type=F32) * ATTN_SCALE + bias
            if qb == 0:
                s = jnp.where(jnp.logical_or(sb > 0, col >= QUERY_BLOCK), s, MASK_VALUE)
            m = jnp.max(s, axis=-1, keepdims=True)
            e = jnp.exp(s - m)
            den = jnp.sum(e, axis=-1, keepdims=True)
            o_scr[rows(start, QUERY_BLOCK), :] = jnp.dot((e / den).astype(BF16), vu, preferred_element_type=F32)
            l_scr[rows(start, QUERY_BLOCK), :] = jnp.broadcast_to(m + jnp.log(den), (QUERY_BLOCK, HEAD_DIM))

    o_ref[...] = o_scr[...].astype(BF16)
    lse_ref[...] = l_scr[...]
    kc_scr[0:win, :] = kc_scr[sup:sup + win, :]
    vc_scr[0:win, :] = vc_scr[sup:sup + win, :]


def _attn_prompt(z, band, gi, batch, seq):
    d = GROUPS[gi][1]
    sup = ATTN_SUPER
    win = QUERY_BLOCK * d
    nsb = seq // sup
    qb0 = COL_Q // HEAD_DIM + gi * HEADS_PER_GROUP
    kb0 = COL_K // HEAD_DIM + gi * HEADS_PER_GROUP
    vb0 = COL_V // HEAD_DIM + gi * HEADS_PER_GROUP
    blk = lambda c0: pl.BlockSpec((sup, HEAD_DIM), lambda b, h, s: (b * nsb + s, c0 + h))
    out = pl.BlockSpec((sup, HEAD_DIM), lambda b, h, s: (b * nsb + s, h))
    return pl.pallas_call(
        functools.partial(_attn_kernel, d=d, sup=sup),
        grid=(batch, HEADS_PER_GROUP, nsb),
        in_specs=[blk(qb0), blk(kb0), blk(vb0),
                  pl.BlockSpec((None, 1, BAND_LANES), lambda b, h, s: (h, 0, 0))],
        out_specs=[out, out],
        out_shape=[jax.ShapeDtypeStruct((batch * seq, D_ATTN_OUT), BF16),
                   jax.ShapeDtypeStruct((batch * seq, D_ATTN_OUT), F32)],
        scratch_shapes=[pltpu.VMEM((sup, HEAD_DIM), F32), pltpu.VMEM((win + sup, HEAD_DIM), F32),
                        pltpu.VMEM((win + sup, HEAD_DIM), F32), pltpu.VMEM((sup, HEAD_DIM), F32),
                        pltpu.VMEM((sup, HEAD_DIM), F32)],
        compiler_params=_cparams(3),
        name=f"attn_g{gi}",
    )(z, z, z, band)


def _attn_step_kernel(q_ref, kn_ref, vn_ref, c0_ref, c1_ref, c2_ref, bc_ref, bn_ref, o_ref, lse_ref, *, bb):
    caches = (c0_ref, c1_ref, c2_ref)
    for b in range(bb):
        for g in range(len(GROUPS)):
            hs = slice(g * HEADS_PER_GROUP, (g + 1) * HEADS_PER_GROUP)
            qg = _round_bf16(q_ref[b, hs, :])
            kc = _round_bf16(caches[g][b, :, 0])
            vc = _round_bf16(caches[g][b, :, 1])
            kn = _round_bf16(kn_ref[b, hs, :])
            vn = _round_bf16(vn_ref[b, hs, :])
            sc = jnp.sum(kc * qg[None], axis=-1, keepdims=True) * ATTN_SCALE + bc_ref[g]
            sn = jnp.sum(kn * qg, axis=-1, keepdims=True) * ATTN_SCALE + bn_ref[g]
            m = jnp.maximum(jnp.max(sc, axis=0), sn)
            ec = jnp.exp(sc - m[None])
            en = jnp.exp(sn - m)
            den = jnp.sum(ec, axis=0) + en
            o = jnp.sum(_round_bf16(ec / den[None]) * vc, axis=0) + _round_bf16(en / den) * vn
            o_ref[g, b] = o
            lse_ref[g, b] = jnp.broadcast_to(m + jnp.log(den), (HEADS_PER_GROUP, HEAD_DIM))


def _attn_step(q, kn, vn, caches, bias_c, bias_n, bb):
    n = q.shape[0]
    ng = len(GROUPS)
    tok = pl.BlockSpec((bb, N_ATTN_HEADS, HEAD_DIM), lambda i: (i, 0, 0))
    cache = pl.BlockSpec((bb, QUERY_BLOCK, None, 2, HEADS_PER_GROUP, HEAD_DIM), lambda i: (i, 0, 0, 0, 0, 0))
    out = pl.BlockSpec((ng, bb, HEADS_PER_GROUP, HEAD_DIM), lambda i: (0, i, 0, 0))
    return pl.pallas_call(
        functools.partial(_attn_step_kernel, bb=bb),
        grid=(n // bb,),
        in_specs=[tok, tok, tok, cache, cache, cache,
                  pl.BlockSpec((ng, QUERY_BLOCK, HEADS_PER_GROUP, 1), lambda i: (0, 0, 0, 0)),
                  pl.BlockSpec((ng, HEADS_PER_GROUP, 1), lambda i: (0, 0, 0))],
        out_specs=[out, out],
        out_shape=[jax.ShapeDtypeStruct((ng, n, HEADS_PER_GROUP, HEAD_DIM), F32)] * 2,
        compiler_params=_cparams(1),
        name="attn_step",
    )(q, kn, vn, *caches, bias_c, bias_n)


def _mix_kernel(a_ref, o0_ref, o1_ref, o2_ref, l0_ref, l1_ref, l2_ref, ga_ref, gb_ref, x_ref,
                wr_ref, wa_ref, wo_ref, gf_ref, wrt_ref, brt_ref, cnt0_ref,
                x1_ref, xn_ref, gate_ref, dest_ref, cnt_ref, cnt_scr, *, cap, sub):

    @pl.when(pl.program_id(0) == 0)
    def _():
        cnt_scr[...] = cnt0_ref[...].astype(F32)

    for s in range(x_ref.shape[0] // sub):
        _mix_rows(slice(s * sub, (s + 1) * sub), a_ref, o0_ref, o1_ref, o2_ref, l0_ref, l1_ref, l2_ref, ga_ref,
                  gb_ref, x_ref, wr_ref, wa_ref, wo_ref, gf_ref, wrt_ref, brt_ref,
                  x1_ref, xn_ref, gate_ref, dest_ref, cnt_scr, cap)
    cnt_ref[...] = cnt_scr[...].astype(jnp.int32)


def _mix_rows(rows, a_ref, o0_ref, o1_ref, o2_ref, l0_ref, l1_ref, l2_ref, ga_ref, gb_ref, x_ref,
              wr_ref, wa_ref, wo_ref, gf_ref, wrt_ref, brt_ref, x1_ref, xn_ref, gate_ref, dest_ref, cnt_scr, cap):
    tm = rows.stop - rows.start
    l0, l1, l2 = l0_ref[rows, :], l1_ref[rows, :], l2_ref[rows, :]
    mx = jnp.maximum(jnp.maximum(l0, l1), l2)
    e0, e1, e2 = jnp.exp(l0 - mx), jnp.exp(l1 - mx), jnp.exp(l2 - mx)
    den = e0 + e1 + e2
    b_out = (_round_bf16(e0 / den) * _round_bf16(o0_ref[rows, :]) + _round_bf16(e1 / den) * _round_bf16(o1_ref[rows, :])
             + _round_bf16(e2 / den) * _round_bf16(o2_ref[rows, :]))
    rnn = jnp.dot(a_ref[rows, :], wr_ref[...], preferred_element_type=F32)
    att = jnp.dot(b_out.astype(BF16), wa_ref[...], preferred_element_type=F32)
    merged = (jax.nn.sigmoid(ga_ref[rows, :].astype(F32)) * rnn
              + jax.nn.sigmoid(gb_ref[rows, :].astype(F32)) * att)
    x1 = x_ref[rows, :] + jnp.dot(merged.astype(BF16), wo_ref[...], preferred_element_type=F32)
    x1_ref[rows, :] = x1
    xn = _rms(x1, gf_ref[...])
    _store_row_tiles(xn_ref.at[pl.ds(rows.start * ROW_TILE, tm * ROW_TILE)], xn)

    logits = jnp.dot(xn.astype(BF16), wrt_ref[...], preferred_element_type=F32) + brt_ref[...]
    lane = lax.broadcasted_iota(jnp.int32, (tm, LANES), 1)
    cur = jnp.where(lane < N_EXPERTS, logits, -jnp.inf)
    vals, idxs = [], []
    for _ in range(TOP_K):
        mk = jnp.max(cur, axis=-1, keepdims=True)
        ik = jnp.min(jnp.where(cur == mk, lane, LANES), axis=-1, keepdims=True)
        vals.append(mk)
        idxs.append(ik)
        cur = jnp.where(lane == ik, -jnp.inf, cur)
    es = [jnp.exp(v - vals[0]) for v in vals]
    den = es[0] + es[1] + es[2] + es[3]
    chosen = jnp.zeros((tm, LANES), F32)
    for k in range(TOP_K):
        chosen = chosen + jnp.where(lane == idxs[k], 1.0, 0.0)
    tri = jnp.where(lax.broadcasted_iota(jnp.int32, (tm, tm), 1) <= lax.broadcasted_iota(jnp.int32, (tm, tm), 0),
                    1.0, 0.0).astype(BF16)
    incl = jnp.dot(tri, chosen.astype(BF16), preferred_element_type=F32)
    rank = cnt_scr[...] + incl - chosen
    gate_out = jnp.zeros((tm, LANES), F32)
    dest_out = jnp.zeros((tm, LANES), jnp.int32)
    for k in range(TOP_K):
        rank_k = jnp.sum(jnp.where(lane == idxs[k], rank, 0.0), axis=-1, keepdims=True).astype(jnp.int32)
        gate_out = jnp.where(lane == k, es[k] / den, gate_out)
        dest_out = jnp.where(lane == k, idxs[k] * cap + rank_k, dest_out)
    gate_ref[rows, :] = gate_out
    if tm % LANES:
        dest_out = jnp.concatenate([dest_out, jnp.zeros((LANES - tm % LANES, LANES), jnp.int32)], axis=0)
    dest_ref[:, rows] = dest_out.T[0:DEST_ROWS, 0:tm]
    cnt_scr[...] = cnt_scr[...] + incl[tm - 1:tm, :]


def _mix(a_out, os_, ls_, z, x2d, wr, wa, wo, gf, wrt, brt, cnt0, tm, cap):
    n = x2d.shape[0]
    row = lambda w, c=0: pl.BlockSpec((tm, w), lambda i: (i, c))
    full = lambda shape: pl.BlockSpec(shape, lambda i: (0,) * len(shape))
    return pl.pallas_call(
        functools.partial(_mix_kernel, cap=cap, sub=min(tm, MIX_SUB)),
        grid=(n // tm,),
        in_specs=[row(D_RNN), row(D_ATTN_OUT), row(D_ATTN_OUT), row(D_ATTN_OUT),
                  row(D_ATTN_OUT), row(D_ATTN_OUT), row(D_ATTN_OUT),
                  row(D_MODEL, COL_GA // D_MODEL), row(D_MODEL, COL_GB // D_MODEL), row(D_MODEL),
                  full((D_RNN, D_MODEL)), full((D_ATTN_OUT, D_MODEL)), full((D_MODEL, D_MODEL)),
                  full((1, D_MODEL)), full((D_MODEL, LANES)), full((1, LANES)), full((1, LANES))],
        out_specs=[row(D_MODEL), pl.BlockSpec((tm * ROW_TILE, LANES), lambda i: (i, 0)), row(LANES),
                   pl.BlockSpec((None, DEST_ROWS, tm), lambda i: (i, 0, 0)), full((1, LANES))],
        out_shape=[jax.ShapeDtypeStruct((n, D_MODEL), F32), jax.ShapeDtypeStruct((n * ROW_TILE, LANES), F32),
                   jax.ShapeDtypeStruct((n, LANES), F32), jax.ShapeDtypeStruct((n // tm, DEST_ROWS, tm), jnp.int32),
                   jax.ShapeDtypeStruct((1, LANES), jnp.int32)],
        scratch_shapes=[pltpu.VMEM((1, LANES), F32)],
        compiler_params=_cparams(1),
        name="mix",
    )(a_out, *os_, *ls_, z, z, x2d, wr, wa, wo, gf, wrt, brt, cnt0)


DMA_UNROLL = 8
DEST_ROWS = 8
MIX_TM = 512
MIX_SUB = 256


def _start_rows(tm, copies):
    def body(r, carry):
        for k, c in enumerate(copies(r)):
            c.start(priority=k % 2)
        return carry

    lax.fori_loop(0, tm, body, 0, unroll=DMA_UNROLL)


def _wait_rows(tm, copies):
    first = copies(0)

    def body(r, carry):
        for c in first:
            c.wait()
        return carry

    lax.fori_loop(0, tm, body, 0, unroll=DMA_UNROLL)


def _dispatch_kernel(dprev_ref, dcur_ref, xn_hbm, *rest, tm):
    xb_ref, src, load_sem, row_sem = rest[-4:]
    i = pl.program_id(0)
    n = pl.num_programs(0)
    slot = i % 2

    def load(tile, s):
        rows = pl.ds(pl.multiple_of(tile * (tm * ROW_TILE), ROW_TILE), tm * ROW_TILE)
        return pltpu.make_async_copy(xn_hbm.at[rows], src.at[s], load_sem.at[s])

    def scatter(dest_ref, s):
        return lambda r: [_row_tile_copy(src.at[s], r, xb_ref, dest_ref[k * tm + r], row_sem.at[s])
                          for k in range(TOP_K)]

    @pl.when(i == 0)
    def _():
        load(0, 0).start()

    @pl.when(i > 0)
    def _():
        _wait_rows(tm, scatter(dprev_ref, 1 - slot))

    @pl.when(i + 1 < n)
    def _():
        load(i + 1, 1 - slot).start()

    load(i, slot).wait()
    _start_rows(tm, scatter(dcur_ref, slot))

    @pl.when(i == n - 1)
    def _():
        _wait_rows(tm, scatter(dcur_ref, slot))


def _dispatch(dest_flat, xn, xb, tm, rows):
    n = xn.shape[0] // ROW_TILE
    any_spec = pl.BlockSpec(memory_space=pl.ANY)
    dest_spec = lambda off: pl.BlockSpec((DEST_ROWS * tm,), lambda i: (jnp.maximum(i + off, 0),),
                                         memory_space=pltpu.SMEM)
    in_specs = [dest_spec(-1), dest_spec(0), any_spec]
    args = [dest_flat, dest_flat, xn]
    aliases = {}
    if xb is not None:
        in_specs.append(any_spec)
        args.append(xb)
        aliases = {3: 0}
    return pl.pallas_call(
        functools.partial(_dispatch_kernel, tm=tm),
        grid=(n // tm,),
        in_specs=in_specs,
        out_specs=any_spec,
        out_shape=jax.ShapeDtypeStruct((rows * ROW_TILE, LANES), F32),
        scratch_shapes=[pltpu.VMEM((2, tm * ROW_TILE, LANES), F32), pltpu.SemaphoreType.DMA((2,)),
                        pltpu.SemaphoreType.DMA((2,))],
        input_output_aliases=aliases,
        compiler_params=_cparams(1),
        name="dispatch",
    )(*args)


def _moe_kernel(be_ref, br_ref, nx_ref, sl_ref, nu_ref, xb_ref, wgu_hbm, bgu_ref, wd_hbm, bd_ref, y_ref,
                wgu_f32, wd_f32, wgu_bf, wd_bf, sems):
    i = pl.program_id(0)
    e = be_ref[i]
    slot = sl_ref[i]
    changed = jnp.logical_or(i == 0, e != be_ref[jnp.maximum(i - 1, 0)])

    def fetch(expert, s):
        return (pltpu.make_async_copy(wgu_hbm.at[expert], wgu_f32.at[s], sems.at[0, s]),
                pltpu.make_async_copy(wd_hbm.at[expert], wd_f32.at[s], sems.at[1, s]))

    @pl.when(i == 0)
    def _():
        for c in fetch(e, slot):
            c.start()

    @pl.when(changed)
    def _():
        @pl.when(nx_ref[i] >= 0)
        def _():
            for c in fetch(nx_ref[i], 1 - slot):
                c.start(priority=1)

        for c in fetch(e, slot):
            c.wait()
        wgu_bf[...] = wgu_f32[slot].astype(BF16)
        wd_bf[...] = wd_f32[slot].astype(BF16)

    @pl.when(i < nu_ref[0])
    def _():
        xb = _load_row_tiles(xb_ref, EXPERT_ROWS).astype(BF16)
        gu = jnp.dot(xb, wgu_bf[...], preferred_element_type=F32) + bgu_ref[...]
        g = jnp.minimum(gu[:, :D_EXPERT], SWIGLU_LIMIT)
        u = jnp.clip(gu[:, D_EXPERT:], -SWIGLU_LIMIT, SWIGLU_LIMIT)
        act = (u + 1.0) * (g * jax.nn.sigmoid(SWIGLU_ALPHA * g))
        _store_row_tiles(y_ref, jnp.dot(act.astype(BF16), wd_bf[...], preferred_element_type=F32) + bd_ref[...])


def _moe(sched, xb, wgu, bgu, wd, bd):
    eb = EXPERT_ROWS
    block_e = sched[0]
    any_spec = pl.BlockSpec(memory_space=pl.ANY)
    grid_spec = pltpu.PrefetchScalarGridSpec(
        num_scalar_prefetch=len(sched),
        grid=(block_e.shape[0],),
        in_specs=[
            pl.BlockSpec((eb * ROW_TILE, LANES), lambda i, be, br, *_: (br[i], 0)),
            any_spec,
            pl.BlockSpec((None, 1, 2 * D_EXPERT), lambda i, be, *_: (be[i], 0, 0)),
            any_spec,
            pl.BlockSpec((None, 1, D_MODEL), lambda i, be, *_: (be[i], 0, 0)),
        ],
        out_specs=pl.BlockSpec((eb * ROW_TILE, LANES), lambda i, be, br, *_: (br[i], 0)),
        scratch_shapes=[pltpu.VMEM((2, D_MODEL, 2 * D_EXPERT), F32), pltpu.VMEM((2, D_EXPERT, D_MODEL), F32),
                        pltpu.VMEM((D_MODEL, 2 * D_EXPERT), BF16), pltpu.VMEM((D_EXPERT, D_MODEL), BF16),
                        pltpu.SemaphoreType.DMA((2, 2))],
    )
    return pl.pallas_call(
        _moe_kernel,
        grid_spec=grid_spec,
        out_shape=jax.ShapeDtypeStruct(xb.shape, F32),
        compiler_params=_cparams(1),
        name="moe",
    )(*sched, xb, wgu, bgu, wd, bd)


def _expert_blocks(counts, cap, n_assign):
    eb = EXPERT_ROWS
    n_grid = -(-n_assign // eb) + N_EXPERTS
    nblk = (counts + eb - 1) // eb
    blk_end = jnp.cumsum(nblk)
    n_used = blk_end[-1:].astype(jnp.int32)
    i = jnp.minimum(jnp.arange(n_grid, dtype=jnp.int32), n_used[0] - 1)
    block_e = jnp.sum(blk_end[None, :] <= i[:, None], axis=1).astype(jnp.int32)
    ids = jnp.arange(N_EXPERTS, dtype=jnp.int32)
    used = nblk > 0
    later_used = jnp.logical_and(used[None, :], ids[None, :] > ids[:, None])
    next_e = jnp.min(jnp.where(later_used, ids[None, :], N_EXPERTS), axis=1)
    next_e = jnp.where(next_e < N_EXPERTS, next_e, -1).astype(jnp.int32)
    slot_e = ((jnp.cumsum(used.astype(jnp.int32)) - 1) % 2).astype(jnp.int32)
    mine = block_e[:, None] == ids[None, :]
    pick = lambda per_expert: jnp.sum(jnp.where(mine, per_expert[None, :], 0), axis=1).astype(jnp.int32)
    block_row = block_e * (cap // eb) + i - pick(blk_end - nblk)
    return (block_e, block_row.astype(jnp.int32), pick(next_e), pick(slot_e), n_used)


def _ple_kernel(dcur_ref, dnext_ref, x1_ref, gate_ref, p_ref, gp_ref, wg_ref, wp_ref, yb_ref, y_ref, ybuf, sems):
    tm = x1_ref.shape[0]
    i = pl.program_id(0)
    slot = i % 2

    def gather(dest_ref, s):
        return lambda r: [_row_tile_copy(yb_ref, dest_ref[k * tm + r], ybuf.at[s, k], r, sems.at[s])
                          for k in range(TOP_K)]

    @pl.when(i == 0)
    def _():
        _start_rows(tm, gather(dcur_ref, 0))

    @pl.when(i + 1 < pl.num_programs(0))
    def _():
        _start_rows(tm, gather(dnext_ref, 1 - slot))

    _wait_rows(tm, gather(dcur_ref, slot))

    gates = gate_ref[...]
    x2 = x1_ref[...]
    for k in range(TOP_K):
        x2 = x2 + _load_row_tiles(ybuf.at[slot, k], tm) * gates[:, k:k + 1]
    gate = jax.nn.sigmoid(jnp.dot(_rms(x2, gp_ref[...]).astype(BF16), wg_ref[...], preferred_element_type=F32))
    proj = jnp.dot(p_ref[...].astype(BF16), wp_ref[...], preferred_element_type=F32)
    y_ref[...] = x2 + gate * proj


def _ple(dest_flat, x1, gates, p2d, gp, wg, wp, yb, tm):
    n = x1.shape[0]
    row = lambda w: pl.BlockSpec((tm, w), lambda i: (i, 0))
    full = lambda shape: pl.BlockSpec(shape, lambda i: (0,) * len(shape))
    last = n // tm - 1
    dest_spec = lambda off: pl.BlockSpec((DEST_ROWS * tm,), lambda i: (jnp.minimum(i + off, last),),
                                         memory_space=pltpu.SMEM)
    return pl.pallas_call(
        _ple_kernel,
        grid=(n // tm,),
        in_specs=[dest_spec(0), dest_spec(1),
                  row(D_MODEL), row(LANES), row(D_PLE), full((1, D_MODEL)),
                  full((D_MODEL, D_MODEL)), full((D_PLE, D_MODEL)), pl.BlockSpec(memory_space=pl.ANY)],
        out_specs=row(D_MODEL),
        out_shape=jax.ShapeDtypeStruct((n, D_MODEL), F32),
        scratch_shapes=[pltpu.VMEM((2, TOP_K, tm * ROW_TILE, LANES), F32), pltpu.SemaphoreType.DMA((2,))],
        compiler_params=_cparams(1),
        name="ple",
    )(dest_flat, dest_flat, x1, gates, p2d, gp, wg, wp, yb)


def kernel(x_prompt, x_sample, state_conv, state_h, cache_kv_w128, cache_kv_w512, cache_kv_w2048, p_prompt,
           p_sample, norm_mix_g, w_in, conv_w, conv_b, w_rg_a, b_rg_a, w_rg_x, b_rg_x, lru_lambda, q_norm_g,
           k_norm_g, rel_bias, w_br_rnn, w_br_attn, w_out, norm_ffn_g, w_router, b_router, w_gate_up, b_gate_up,
           w_down, b_down, norm_ple_g, w_ple_gate, w_ple_proj):
    bp, seq, _ = x_prompt.shape
    bs = x_sample.shape[0]
    n_p = bp * seq
    li = 0
    row = lambda v: v.reshape(1, -1).astype(F32)

    w = w_in[li]
    o_q, o_ga = 2 * D_RNN, 2 * D_RNN + 3 * D_QKV
    w_perm = jnp.concatenate([w[:, :o_q], w[:, o_ga:], w[:, o_q:o_ga]], axis=1).astype(BF16)
    colgain = jnp.concatenate([jnp.ones((COL_Q,), F32), jnp.tile(q_norm_g[li], N_ATTN_HEADS),
                               jnp.tile(k_norm_g[li], N_ATTN_HEADS), jnp.ones((D_QKV,), F32)]).reshape(1, D_IN)
    wax = jnp.concatenate([w_rg_a[li], w_rg_x[li]], axis=-1).astype(BF16)
    lru = (conv_w[li].astype(F32), row(conv_b[li]), wax, row(b_rg_a[li]), row(b_rg_x[li]), row(lru_lambda[li]))
    wr, wa, wo = w_br_rnn[li].astype(BF16), w_br_attn[li].astype(BF16), w_out[li].astype(BF16)
    wrt = jnp.pad(w_router[li].astype(BF16), ((0, 0), (0, LANES - N_EXPERTS)))
    brt = jnp.pad(row(b_router[li]), ((0, 0), (0, LANES - N_EXPERTS)))
    gf = row(norm_ffn_g[li])
    biases = [_group_bias(rel_bias, gi, wnd, dil) for gi, (wnd, dil) in enumerate(GROUPS)]

    xp = x_prompt.reshape(n_p, D_MODEL)
    zp = _inproj(xp, row(norm_mix_g[li]), w_perm, colgain, tm=1024, out_dtype=BF16)
    a_p, h_p = _rglru_prompt(zp, bp, seq, 512, *lru)
    os_p, ls_p = [], []
    for gi in range(len(GROUPS)):
        o, l = _attn_prompt(zp, _band_bias(biases[gi]), gi, bp, seq)
        os_p.append(o)
        ls_p.append(l)
    cap = -(-(n_p + bs) // EXPERT_ROWS) * EXPERT_ROWS
    x1_p, xn_p, gate_p, dest_p, cnt_p = _mix(a_p, os_p, ls_p, zp, xp, wr, wa, wo, gf, wrt, brt,
                                              jnp.zeros((1, LANES), jnp.int32), tm=MIX_TM, cap=cap)

    xs = x_sample.reshape(bs, D_MODEL)
    zs = _inproj(xs, row(norm_mix_g[li]), w_perm, colgain, tm=bs, out_dtype=F32)
    sc = state_conv[li]
    a_s, h_s = _rglru_step(zs, [sc[:, j] for j in range(CONV_WIDTH - 1)], state_h[li], *lru)
    heads = lambda c0: zs[:, c0:c0 + D_QKV].astype(F32).reshape(bs, N_ATTN_HEADS, HEAD_DIM)
    caches = [c[li].reshape(bs, QUERY_BLOCK, dil, 2, HEADS_PER_GROUP, HEAD_DIM)
              for c, (_, dil) in zip((cache_kv_w128, cache_kv_w512, cache_kv_w2048), GROUPS)]
    bias_c = jnp.stack([b[:, :0:-1].T for b in biases])[..., None]
    bias_n = jnp.stack([b[:, 0] for b in biases])[..., None]
    o_s, l_s = _attn_step(heads(COL_Q), heads(COL_K), heads(COL_V), caches, bias_c, bias_n, bb=4)
    os_s = [o_s[g].reshape(bs, D_ATTN_OUT) for g in range(len(GROUPS))]
    ls_s = [l_s[g].reshape(bs, D_ATTN_OUT) for g in range(len(GROUPS))]
    x1_s, xn_s, gate_s, dest_s, cnt_s = _mix(a_s, os_s, ls_s, zs, xs, wr, wa, wo, gf, wrt, brt, cnt_p, tm=bs, cap=cap)

    dest_p, dest_s = dest_p.reshape(-1), dest_s.reshape(-1)
    xb = _dispatch(dest_p, xn_p, None, MIX_TM, N_EXPERTS * cap)
    xb = _dispatch(dest_s, xn_s, xb, bs, N_EXPERTS * cap)
    sched = _expert_blocks(cnt_s[0, :N_EXPERTS], cap, (n_p + bs) * TOP_K)
    yb = _moe(sched, xb, w_gate_up[li], b_gate_up[li][:, None, :], w_down[li], b_down[li][:, None, :])

    gp, wg, wp = row(norm_ple_g[li]), w_ple_gate[li].astype(BF16), w_ple_proj[li].astype(BF16)
    y_p = _ple(dest_p, x1_p, gate_p, p_prompt[li].reshape(n_p, D_PLE), gp, wg, wp, yb, tm=MIX_TM)
    y_s = _ple(dest_s, x1_s, gate_s, p_sample[li].reshape(bs, D_PLE), gp, wg, wp, yb, tm=bs)

    z3 = zp.reshape(bp, seq, D_IN)
    zs3 = zs.reshape(bs, 1, D_IN)

    def new_kv(zz, gi, keep):
        t = zz.shape[1]
        c = gi * D_ATTN_OUT
        kk = zz[:, t - keep:, COL_K + c:COL_K + c + D_ATTN_OUT].astype(F32)
        vv = zz[:, t - keep:, COL_V + c:COL_V + c + D_ATTN_OUT].astype(F32)
        shape = (zz.shape[0], keep, HEADS_PER_GROUP, HEAD_DIM)
        return jnp.stack([kk.reshape(shape), vv.reshape(shape)], axis=2)[None]

    conv_p = z3[:, seq - (CONV_WIDTH - 1):, COL_XR:COL_XR + D_RNN].astype(F32)[None]
    conv_s = jnp.concatenate([sc[:, 1:], zs3[:, :, COL_XR:COL_XR + D_RNN].astype(F32)], axis=1)[None]
    outs = [y_p.reshape(bp, seq, D_MODEL), y_s.reshape(bs, 1, D_MODEL), conv_p, conv_s,
            h_p.reshape(1, bp, D_RNN), h_s.reshape(1, bs, D_RNN)]
    for gi, (wnd, _) in enumerate(GROUPS):
        outs.append(new_kv(z3, gi, min(wnd, seq)))
        outs.append(new_kv(zs3, gi, 1))
    return tuple(outs)
```

```python
import functools
import math

import numpy as np
import jax
import jax.numpy as jnp
from jax import lax
from jax.experimental import pallas as pl
from jax.experimental.pallas import tpu as pltpu

F32 = jnp.float32
BF16 = jnp.bfloat16

D_MODEL = 1024
D_RNN = 1024
N_RNN_BLOCKS = 8
RNN_BLOCK = D_RNN // N_RNN_BLOCKS
CONV_WIDTH = 4
LRU_C = 8.0
GROUPS = ((128, 1), (512, 4), (2048, 16))
HEADS_PER_GROUP = 4
HEAD_DIM = 128
N_ATTN_HEADS = HEADS_PER_GROUP * len(GROUPS)
D_QKV = N_ATTN_HEADS * HEAD_DIM
D_ATTN_OUT = HEADS_PER_GROUP * HEAD_DIM
N_BUCKETS = 32
MAX_DISTANCE = 2048
QUERY_BLOCK = 128
D_IN = 2 * D_RNN + 3 * D_QKV + 2 * D_MODEL
N_EXPERTS = 32
TOP_K = 4
D_EXPERT = 1024
SWIGLU_ALPHA = 1.702
SWIGLU_LIMIT = 7.0
D_PLE = 256
EPS = 1e-6
ATTN_SCALE = HEAD_DIM ** -0.5
MASK_VALUE = -1e30

LANES = 128
ROW_TILE = D_MODEL // LANES
COL_XR, COL_YG, COL_GA, COL_GB = 0, D_RNN, 2 * D_RNN, 2 * D_RNN + D_MODEL
COL_Q = 2 * D_RNN + 2 * D_MODEL
COL_K = COL_Q + D_QKV
COL_V = COL_K + D_QKV
INPROJ_TN = D_IN // 4
INPROJ_CHUNKS = ((0, 512), (512, 512), (1024, 512), (1536, 640))
INPROJ_SUB = 1024
ATTN_SUPER = 4096
RGLRU_CHUNK = 1024
BAND_LANES = 512
EXPERT_ROWS = 512
VMEM_LIMIT = 56 * 1024 * 1024


def _cparams(n_axes):
    return pltpu.CompilerParams(dimension_semantics=("arbitrary",) * n_axes, vmem_limit_bytes=VMEM_LIMIT)


def _rms(x, g):
    ms = jnp.mean(x * x, axis=-1, keepdims=True)
    return (x * lax.rsqrt(ms + EPS)) * g


def _round_bf16(x):
    return x.astype(BF16).astype(F32)


def _load_row_tiles(ref, n_rows):
    return jnp.concatenate([ref[pl.ds(c, n_rows, stride=ROW_TILE), :] for c in range(ROW_TILE)], axis=1)


def _store_row_tiles(ref, value):
    n_rows = value.shape[0]
    for c in range(ROW_TILE):
        ref[pl.ds(c, n_rows, stride=ROW_TILE), :] = value[:, c * LANES:(c + 1) * LANES]


def _row_tile_copy(src_ref, src_row, dst_ref, dst_row, sem):
    src = src_ref.at[pl.ds(pl.multiple_of(src_row * ROW_TILE, ROW_TILE), ROW_TILE)]
    dst = dst_ref.at[pl.ds(pl.multiple_of(dst_row * ROW_TILE, ROW_TILE), ROW_TILE)]
    return pltpu.make_async_copy(src, dst, sem)


def _inproj_kernel(x_ref, g_ref, w_ref, cg_ref, z_ref, u_scr, *, sub):
    j = pl.program_id(1)
    n_sub = x_ref.shape[0] // sub

    @pl.when(j == 0)
    def _():
        for s in range(n_sub):
            rows = slice(s * sub, (s + 1) * sub)
            u_scr[rows, :] = _rms(x_ref[rows, :], g_ref[...]).astype(BF16)

    for s in range(n_sub):
        rows = slice(s * sub, (s + 1) * sub)
        for c0, width in INPROJ_CHUNKS:
            acc = jnp.dot(u_scr[rows, :], w_ref[:, c0:c0 + width], preferred_element_type=F32)
            for h in range(width // HEAD_DIM):
                cs = slice(c0 + h * HEAD_DIM, c0 + (h + 1) * HEAD_DIM)
                col = j * INPROJ_TN + cs.start
                is_qk = jnp.logical_and(col >= COL_Q, col < COL_V)
                a = acc[:, h * HEAD_DIM:(h + 1) * HEAD_DIM]
                z_ref[rows, cs] = jnp.where(is_qk, _rms(a, cg_ref[:, cs]), a).astype(z_ref.dtype)


def _inproj(x2d, g, w_perm, colgain, tm, out_dtype):
    n = x2d.shape[0]
    return pl.pallas_call(
        functools.partial(_inproj_kernel, sub=min(tm, INPROJ_SUB)),
        grid=(n // tm, D_IN // INPROJ_TN),
        in_specs=[
            pl.BlockSpec((tm, D_MODEL), lambda i, j: (i, 0)),
            pl.BlockSpec((1, D_MODEL), lambda i, j: (0, 0)),
            pl.BlockSpec((D_MODEL, INPROJ_TN), lambda i, j: (0, j)),
            pl.BlockSpec((1, INPROJ_TN), lambda i, j: (0, j)),
        ],
        out_specs=pl.BlockSpec((tm, INPROJ_TN), lambda i, j: (i, j)),
        out_shape=jax.ShapeDtypeStruct((n, D_IN), out_dtype),
        scratch_shapes=[pltpu.VMEM((tm, D_MODEL), BF16)],
        compiler_params=_cparams(2),
        name="inproj",
    )(x2d, g, w_perm, colgain)


GELU_C1 = 2.0 * math.sqrt(2.0 / math.pi)
GELU_C2 = GELU_C1 * 0.044715


def _gelu_tanh(x):
    return x / (1.0 + jnp.exp(x * (-GELU_C1 - GELU_C2 * (x * x))))


def _softplus(v):
    return jnp.maximum(v, 0.0) + jnp.log1p(jnp.exp(-jnp.abs(v)))


def _lru_gates_block(xc, n, wax_ref, ba_ref, bx_ref, lam_ref):
    cs = slice(n * RNN_BLOCK, (n + 1) * RNN_BLOCK)
    ri = jnp.dot(xc.astype(BF16), wax_ref[n], preferred_element_type=F32)
    r = jax.nn.sigmoid(ri[:, :RNN_BLOCK] + ba_ref[:, cs])
    i = jax.nn.sigmoid(ri[:, RNN_BLOCK:] + bx_ref[:, cs])
    log_a = (-LRU_C * r) * _softplus(-lam_ref[:, cs])
    a = jnp.exp(log_a)
    v = 1.0 - a * a
    b = jnp.where(v > 0.0, v * lax.rsqrt(v), 0.0) * (i * xc)
    return a, b


def _rglru_kernel(xr_ref, yg_ref, cw_ref, cb_ref, wax_ref, ba_ref, bx_ref, lam_ref,
                  aout_ref, hlast_ref, xe_scr, a_scr, b_scr, hcar_scr, *, tc):
    c = pl.program_id(1)
    sub = 8

    @pl.when(c == 0)
    def _():
        xe_scr[0:8, :] = jnp.zeros((8, D_RNN), F32)
        hcar_scr[...] = jnp.zeros((sub, D_RNN), F32)

    xe_scr[8:8 + tc, :] = xr_ref[...].astype(F32)
    for n in range(N_RNN_BLOCKS):
        cs = slice(n * RNN_BLOCK, (n + 1) * RNN_BLOCK)
        xc = cb_ref[:, cs]
        for j in range(CONV_WIDTH):
            xc = xc + xe_scr[pl.ds(8 - (CONV_WIDTH - 1) + j, tc), cs] * cw_ref[j:j + 1, cs]
        a, b = _lru_gates_block(xc, n, wax_ref, ba_ref, bx_ref, lam_ref)
        a_scr[n] = a
        b_scr[n] = b
    xe_scr[0:8, :] = xe_scr[tc:tc + 8, :]

    row = lax.broadcasted_iota(jnp.int32, (sub, RNN_BLOCK), 0)

    def step(g, carry):
        rows = pl.ds(pl.multiple_of(g * sub, sub), sub)
        new = []
        for n in range(N_RNN_BLOCKS):
            a = a_scr[n, rows, :]
            b = b_scr[n, rows, :]
            for d in (1, 2, 4):
                keep = row >= d
                b = a * jnp.where(keep, pltpu.roll(b, d, 0), 0.0) + b
                a = a * jnp.where(keep, pltpu.roll(a, d, 0), 1.0)
            h = b + a * carry[n]
            b_scr[n, rows, :] = h
            new.append(jnp.broadcast_to(h[sub - 1:sub, :], (sub, RNN_BLOCK)))
        return tuple(new)

    init = tuple(hcar_scr[:, n * RNN_BLOCK:(n + 1) * RNN_BLOCK] for n in range(N_RNN_BLOCKS))
    h_in = lax.fori_loop(0, tc // sub, step, init)

    for n in range(N_RNN_BLOCKS):
        cs = slice(n * RNN_BLOCK, (n + 1) * RNN_BLOCK)
        aout_ref[:, cs] = (b_scr[n] * _gelu_tanh(yg_ref[:, cs].astype(F32))).astype(BF16)
        hcar_scr[:, cs] = h_in[n]
        hlast_ref[:, cs] = h_in[n][0:1, :]


def _rglru_prompt(z, batch, seq, tc, cw, cb, wax, ba, bx, lam):
    nc = seq // tc
    vec = lambda r: pl.BlockSpec((r, D_RNN), lambda b, c: (0, 0))
    return pl.pallas_call(
        functools.partial(_rglru_kernel, tc=tc),
        grid=(batch, nc),
        in_specs=[
            pl.BlockSpec((tc, D_RNN), lambda b, c: (b * nc + c, COL_XR // D_RNN)),
            pl.BlockSpec((tc, D_RNN), lambda b, c: (b * nc + c, COL_YG // D_RNN)),
            vec(CONV_WIDTH), vec(1),
            pl.BlockSpec((N_RNN_BLOCKS, RNN_BLOCK, 2 * RNN_BLOCK), lambda b, c: (0, 0, 0)),
            vec(1), vec(1), vec(1),
        ],
        out_specs=[
            pl.BlockSpec((tc, D_RNN), lambda b, c: (b * nc + c, 0)),
            pl.BlockSpec((None, 1, D_RNN), lambda b, c: (b, 0, 0)),
        ],
        out_shape=[jax.ShapeDtypeStruct((batch * seq, D_RNN), BF16),
                   jax.ShapeDtypeStruct((batch, 1, D_RNN), F32)],
        scratch_shapes=[pltpu.VMEM((tc + 8, D_RNN), F32)]
        + [pltpu.VMEM((N_RNN_BLOCKS, tc, RNN_BLOCK), F32)] * 2
        + [pltpu.VMEM((8, D_RNN), F32)],
        compiler_params=_cparams(2),
        name="rglru",
    )(z, z, cw, cb, wax, ba, bx, lam)


def _rglru_step_kernel(xr_ref, yg_ref, c0_ref, c1_ref, c2_ref, h0_ref, cw_ref, cb_ref, wax_ref, ba_ref,
                       bx_ref, lam_ref, aout_ref, hout_ref):
    taps = (c0_ref, c1_ref, c2_ref)
    for n in range(N_RNN_BLOCKS):
        cs = slice(n * RNN_BLOCK, (n + 1) * RNN_BLOCK)
        xc = cb_ref[:, cs]
        for j in range(CONV_WIDTH - 1):
            xc = xc + taps[j][:, cs] * cw_ref[j:j + 1, cs]
        xc = xc + xr_ref[:, cs].astype(F32) * cw_ref[CONV_WIDTH - 1:CONV_WIDTH, cs]
        a, b = _lru_gates_block(xc, n, wax_ref, ba_ref, bx_ref, lam_ref)
        h = a * h0_ref[:, cs] + b
        hout_ref[:, cs] = h
        aout_ref[:, cs] = (h * _gelu_tanh(yg_ref[:, cs].astype(F32))).astype(BF16)


def _rglru_step(z, conv_taps, h0, cw, cb, wax, ba, bx, lam):
    n = z.shape[0]
    full = lambda shape: pl.BlockSpec(shape, lambda i: (0,) * len(shape))
    return pl.pallas_call(
        _rglru_step_kernel,
        grid=(1,),
        in_specs=[
            pl.BlockSpec((n, D_RNN), lambda i: (0, COL_XR // D_RNN)),
            pl.BlockSpec((n, D_RNN), lambda i: (0, COL_YG // D_RNN)),
            full((n, D_RNN)), full((n, D_RNN)), full((n, D_RNN)), full((n, D_RNN)),
            full((CONV_WIDTH, D_RNN)), full((1, D_RNN)),
            full((N_RNN_BLOCKS, RNN_BLOCK, 2 * RNN_BLOCK)),
            full((1, D_RNN)), full((1, D_RNN)), full((1, D_RNN)),
        ],
        out_specs=[full((n, D_RNN)), full((n, D_RNN))],
        out_shape=[jax.ShapeDtypeStruct((n, D_RNN), BF16), jax.ShapeDtypeStruct((n, D_RNN), F32)],
        compiler_params=_cparams(1),
        name="rglru_step",
    )(z, z, *conv_taps, h0, cw, cb, wax, ba, bx, lam)


def _t5_causal_buckets(dist):
    max_exact = N_BUCKETS // 2
    d = np.maximum(dist, 1).astype(np.float32)
    large = max_exact + (np.log(d / max_exact) / np.log(MAX_DISTANCE / max_exact)
                         * (N_BUCKETS - max_exact)).astype(np.int32)
    large = np.minimum(large, N_BUCKETS - 1)
    return np.where(dist < max_exact, dist, large).astype(np.int32)


def _group_bias(rel_bias, gi, window, dilation):
    buckets = _t5_causal_buckets(np.arange(window // dilation + 1) * dilation)
    heads = slice(gi * HEADS_PER_GROUP, (gi + 1) * HEADS_PER_GROUP)
    return rel_bias[buckets][:, heads].T.astype(F32)


def _band_bias(bias):
    nh, nk = bias.shape
    assert nk == QUERY_BLOCK + 1
    return jnp.concatenate([bias[:, ::-1], jnp.full((nh, BAND_LANES - nk), MASK_VALUE, F32)], axis=1)[:, None, :]


def _attn_kernel(q_ref, k_ref, v_ref, bias_ref, o_ref, lse_ref, q_scr, kc_scr, vc_scr, o_scr, l_scr, *, d, sup):
    sb = pl.program_id(2)
    win = QUERY_BLOCK * d

    @pl.when(sb == 0)
    def _():
        kc_scr[0:win, :] = jnp.zeros((win, HEAD_DIM), F32)
        vc_scr[0:win, :] = jnp.zeros((win, HEAD_DIM), F32)

    q_scr[...] = q_ref[...].astype(F32)
    kc_scr[win:win + sup, :] = k_ref[...].astype(F32)
    vc_scr[win:win + sup, :] = v_ref[...].astype(F32)
    bias = pltpu.roll(jnp.broadcast_to(bias_ref[...], (QUERY_BLOCK, BAND_LANES)), 0, 1, stride=1,
                      stride_axis=0)[:, :2 * QUERY_BLOCK]
    col = lax.broadcasted_iota(jnp.int32, (QUERY_BLOCK, 2 * QUERY_BLOCK), 1)

    def rows(start, size):
        return pl.ds(start, size) if d == 1 else pl.ds(start, size, stride=d)

    for r in range(d):
        for qb in range(sup // win):
            start = r + win * qb
            qu = q_scr[rows(start, QUERY_BLOCK), :].astype(BF16)
            ku = kc_scr[rows(start, 2 * QUERY_BLOCK), :].astype(BF16)
            vu = vc_scr[rows(start, 2 * QUERY_BLOCK), :].astype(BF16)
            s = lax.dot_general(qu, ku, (((1,), (1,)), ((), ())), preferred_element_type=F32) * ATTN_SCALE + bias
            if qb == 0:
                s = jnp.where(jnp.logical_or(sb > 0, col >= QUERY_BLOCK), s, MASK_VALUE)
            m = jnp.max(s, axis=-1, keepdims=True)
            e = jnp.exp(s - m)
            den = jnp.sum(e, axis=-1, keepdims=True)
            o_scr[rows(start, QUERY_BLOCK), :] = jnp.dot((e / den).astype(BF16), vu, preferred_element_type=F32)
            l_scr[rows(start, QUERY_BLOCK), :] = jnp.broadcast_to(m + jnp.log(den), (QUERY_BLOCK, HEAD_DIM))

    o_ref[...] = o_scr[...].astype(BF16)
    lse_ref[...] = l_scr[...]
    kc_scr[0:win, :] = kc_scr[sup:sup + win, :]
    vc_scr[0:win, :] = vc_scr[sup:sup + win, :]


def _attn_prompt(z, band, gi, batch, seq):
    d = GROUPS[gi][1]
    sup = ATTN_SUPER
    win = QUERY_BLOCK * d
    nsb = seq // sup
    qb0 = COL_Q // HEAD_DIM + gi * HEADS_PER_GROUP
    kb0 = COL_K // HEAD_DIM + gi * HEADS_PER_GROUP
    vb0 = COL_V // HEAD_DIM + gi * HEADS_PER_GROUP
    blk = lambda c0: pl.BlockSpec((sup, HEAD_DIM), lambda b, h, s: (b * nsb + s, c0 + h))
    out = pl.BlockSpec((sup, HEAD_DIM), lambda b, h, s: (b * nsb + s, h))
    return pl.pallas_call(
        functools.partial(_attn_kernel, d=d, sup=sup),
        grid=(batch, HEADS_PER_GROUP, nsb),
        in_specs=[blk(qb0), blk(kb0), blk(vb0),
                  pl.BlockSpec((None, 1, BAND_LANES), lambda b, h, s: (h, 0, 0))],
        out_specs=[out, out],
        out_shape=[jax.ShapeDtypeStruct((batch * seq, D_ATTN_OUT), BF16),
                   jax.ShapeDtypeStruct((batch * seq, D_ATTN_OUT), F32)],
        scratch_shapes=[pltpu.VMEM((sup, HEAD_DIM), F32), pltpu.VMEM((win + sup, HEAD_DIM), F32),
                        pltpu.VMEM((win + sup, HEAD_DIM), F32), pltpu.VMEM((sup, HEAD_DIM), F32),
                        pltpu.VMEM((sup, HEAD_DIM), F32)],
        compiler_params=_cparams(3),
        name=f"attn_g{gi}",
    )(z, z, z, band)


def _attn_step_kernel(q_ref, kn_ref, vn_ref, c0_ref, c1_ref, c2_ref, bc_ref, bn_ref, o_ref, lse_ref, *, bb):
    caches = (c0_ref, c1_ref, c2_ref)
    for b in range(bb):
        for g in range(len(GROUPS)):
            hs = slice(g * HEADS_PER_GROUP, (g + 1) * HEADS_PER_GROUP)
            qg = _round_bf16(q_ref[b, hs, :])
            kc = _round_bf16(caches[g][b, :, 0])
            vc = _round_bf16(caches[g][b, :, 1])
            kn = _round_bf16(kn_ref[b, hs, :])
            vn = _round_bf16(vn_ref[b, hs, :])
            sc = jnp.sum(kc * qg[None], axis=-1, keepdims=True) * ATTN_SCALE + bc_ref[g]
            sn = jnp.sum(kn * qg, axis=-1, keepdims=True) * ATTN_SCALE + bn_ref[g]
            m = jnp.maximum(jnp.max(sc, axis=0), sn)
            ec = jnp.exp(sc - m[None])
            en = jnp.exp(sn - m)
            den = jnp.sum(ec, axis=0) + en
            o = jnp.sum(_round_bf16(ec / den[None]) * vc, axis=0) + _round_bf16(en / den) * vn
            o_ref[g, b] = o
            lse_ref[g, b] = jnp.broadcast_to(m + jnp.log(den), (HEADS_PER_GROUP, HEAD_DIM))


def _attn_step(q, kn, vn, caches, bias_c, bias_n, bb):
    n = q.shape[0]
    ng = len(GROUPS)
    tok = pl.BlockSpec((bb, N_ATTN_HEADS, HEAD_DIM), lambda i: (i, 0, 0))
    cache = pl.BlockSpec((bb, QUERY_BLOCK, None, 2, HEADS_PER_GROUP, HEAD_DIM), lambda i: (i, 0, 0, 0, 0, 0))
    out = pl.BlockSpec((ng, bb, HEADS_PER_GROUP, HEAD_DIM), lambda i: (0, i, 0, 0))
    return pl.pallas_call(
        functools.partial(_attn_step_kernel, bb=bb),
        grid=(n // bb,),
        in_specs=[tok, tok, tok, cache, cache, cache,
                  pl.BlockSpec((ng, QUERY_BLOCK, HEADS_PER_GROUP, 1), lambda i: (0, 0, 0, 0)),
                  pl.BlockSpec((ng, HEADS_PER_GROUP, 1), lambda i: (0, 0, 0))],
        out_specs=[out, out],
        out_shape=[jax.ShapeDtypeStruct((ng, n, HEADS_PER_GROUP, HEAD_DIM), F32)] * 2,
        compiler_params=_cparams(1),
        name="attn_step",
    )(q, kn, vn, *caches, bias_c, bias_n)


def _mix_kernel(a_ref, o0_ref, o1_ref, o2_ref, l0_ref, l1_ref, l2_ref, ga_ref, gb_ref, x_ref,
                wr_ref, wa_ref, wo_ref, gf_ref, wrt_ref, brt_ref, cnt0_ref,
                x1_ref, xn_ref, gate_ref, dest_ref, cnt_ref, cnt_scr, *, cap, sub):

    @pl.when(pl.program_id(0) == 0)
    def _():
        cnt_scr[...] = cnt0_ref[...].astype(F32)

    for s in range(x_ref.shape[0] // sub):
        _mix_rows(slice(s * sub, (s + 1) * sub), a_ref, o0_ref, o1_ref, o2_ref, l0_ref, l1_ref, l2_ref, ga_ref,
                  gb_ref, x_ref, wr_ref, wa_ref, wo_ref, gf_ref, wrt_ref, brt_ref,
                  x1_ref, xn_ref, gate_ref, dest_ref, cnt_scr, cap)
    cnt_ref[...] = cnt_scr[...].astype(jnp.int32)


def _mix_rows(rows, a_ref, o0_ref, o1_ref, o2_ref, l0_ref, l1_ref, l2_ref, ga_ref, gb_ref, x_ref,
              wr_ref, wa_ref, wo_ref, gf_ref, wrt_ref, brt_ref, x1_ref, xn_ref, gate_ref, dest_ref, cnt_scr, cap):
    tm = rows.stop - rows.start
    l0, l1, l2 = l0_ref[rows, :], l1_ref[rows, :], l2_ref[rows, :]
    mx = jnp.maximum(jnp.maximum(l0, l1), l2)
    e0, e1, e2 = jnp.exp(l0 - mx), jnp.exp(l1 - mx), jnp.exp(l2 - mx)
    den = e0 + e1 + e2
    b_out = (_round_bf16(e0 / den) * _round_bf16(o0_ref[rows, :]) + _round_bf16(e1 / den) * _round_bf16(o1_ref[rows, :])
             + _round_bf16(e2 / den) * _round_bf16(o2_ref[rows, :]))
    rnn = jnp.dot(a_ref[rows, :], wr_ref[...], preferred_element_type=F32)
    att = jnp.dot(b_out.astype(BF16), wa_ref[...], preferred_element_type=F32)
    merged = (jax.nn.sigmoid(ga_ref[rows, :].astype(F32)) * rnn
              + jax.nn.sigmoid(gb_ref[rows, :].astype(F32)) * att)
    x1 = x_ref[rows, :] + jnp.dot(merged.astype(BF16), wo_ref[...], preferred_element_type=F32)
    x1_ref[rows, :] = x1
    xn = _rms(x1, gf_ref[...])
    _store_row_tiles(xn_ref.at[pl.ds(rows.start * ROW_TILE, tm * ROW_TILE)], xn)

    logits = jnp.dot(xn.astype(BF16), wrt_ref[...], preferred_element_type=F32) + brt_ref[...]
    lane = lax.broadcasted_iota(jnp.int32, (tm, LANES), 1)
    cur = jnp.where(lane < N_EXPERTS, logits, -jnp.inf)
    vals, idxs = [], []
    for _ in range(TOP_K):
        mk = jnp.max(cur, axis=-1, keepdims=True)
        ik = jnp.min(jnp.where(cur == mk, lane, LANES), axis=-1, keepdims=True)
        vals.append(mk)
        idxs.append(ik)
        cur = jnp.where(lane == ik, -jnp.inf, cur)
    es = [jnp.exp(v - vals[0]) for v in vals]
    den = es[0] + es[1] + es[2] + es[3]
    chosen = jnp.zeros((tm, LANES), F32)
    for k in range(TOP_K):
        chosen = chosen + jnp.where(lane == idxs[k], 1.0, 0.0)
    tri = jnp.where(lax.broadcasted_iota(jnp.int32, (tm, tm), 1) <= lax.broadcasted_iota(jnp.int32, (tm, tm), 0),
                    1.0, 0.0).astype(BF16)
    incl = jnp.dot(tri, chosen.astype(BF16), preferred_element_type=F32)
    rank = cnt_scr[...] + incl - chosen
    gate_out = jnp.zeros((tm, LANES), F32)
    dest_out = jnp.zeros((tm, LANES), jnp.int32)
    for k in range(TOP_K):
        rank_k = jnp.sum(jnp.where(lane == idxs[k], rank, 0.0), axis=-1, keepdims=True).astype(jnp.int32)
        gate_out = jnp.where(lane == k, es[k] / den, gate_out)
        dest_out = jnp.where(lane == k, idxs[k] * cap + rank_k, dest_out)
    gate_ref[rows, :] = gate_out
    if tm % LANES:
        dest_out = jnp.concatenate([dest_out, jnp.zeros((LANES - tm % LANES, LANES), jnp.int32)], axis=0)
    dest_ref[:, rows] = dest_out.T[0:DEST_ROWS, 0:tm]
    cnt_scr[...] = cnt_scr[...] + incl[tm - 1:tm, :]


def _mix(a_out, os_, ls_, z, x2d, wr, wa, wo, gf, wrt, brt, cnt0, tm, cap):
    n = x2d.shape[0]
    row = lambda w, c=0: pl.BlockSpec((tm, w), lambda i: (i, c))
    full = lambda shape: pl.BlockSpec(shape, lambda i: (0,) * len(shape))
    return pl.pallas_call(
        functools.partial(_mix_kernel, cap=cap, sub=min(tm, MIX_SUB)),
        grid=(n // tm,),
        in_specs=[row(D_RNN), row(D_ATTN_OUT), row(D_ATTN_OUT), row(D_ATTN_OUT),
                  row(D_ATTN_OUT), row(D_ATTN_OUT), row(D_ATTN_OUT),
                  row(D_MODEL, COL_GA // D_MODEL), row(D_MODEL, COL_GB // D_MODEL), row(D_MODEL),
                  full((D_RNN, D_MODEL)), full((D_ATTN_OUT, D_MODEL)), full((D_MODEL, D_MODEL)),
                  full((1, D_MODEL)), full((D_MODEL, LANES)), full((1, LANES)), full((1, LANES))],
        out_specs=[row(D_MODEL), pl.BlockSpec((tm * ROW_TILE, LANES), lambda i: (i, 0)), row(LANES),
                   pl.BlockSpec((None, DEST_ROWS, tm), lambda i: (i, 0, 0)), full((1, LANES))],
        out_shape=[jax.ShapeDtypeStruct((n, D_MODEL), F32), jax.ShapeDtypeStruct((n * ROW_TILE, LANES), F32),
                   jax.ShapeDtypeStruct((n, LANES), F32), jax.ShapeDtypeStruct((n // tm, DEST_ROWS, tm), jnp.int32),
                   jax.ShapeDtypeStruct((1, LANES), jnp.int32)],
        scratch_shapes=[pltpu.VMEM((1, LANES), F32)],
        compiler_params=_cparams(1),
        name="mix",
    )(a_out, *os_, *ls_, z, z, x2d, wr, wa, wo, gf, wrt, brt, cnt0)


DMA_UNROLL = 8
DEST_ROWS = 8
MIX_TM = 512
MIX_SUB = 256


def _start_rows(tm, copies):
    def body(r, carry):
        for k, c in enumerate(copies(r)):
            c.start(priority=k % 2)
        return carry

    lax.fori_loop(0, tm, body, 0, unroll=DMA_UNROLL)


def _wait_rows(tm, copies):
    first = copies(0)

    def body(r, carry):
        for c in first:
            c.wait()
        return carry

    lax.fori_loop(0, tm, body, 0, unroll=DMA_UNROLL)


def _dispatch_kernel(dprev_ref, dcur_ref, xn_hbm, *rest, tm):
    xb_ref, src, load_sem, row_sem = rest[-4:]
    i = pl.program_id(0)
    n = pl.num_programs(0)
    slot = i % 2

    def load(tile, s):
        rows = pl.ds(pl.multiple_of(tile * (tm * ROW_TILE), ROW_TILE), tm * ROW_TILE)
        return pltpu.make_async_copy(xn_hbm.at[rows], src.at[s], load_sem.at[s])

    def scatter(dest_ref, s):
        return lambda r: [_row_tile_copy(src.at[s], r, xb_ref, dest_ref[k * tm + r], row_sem.at[s])
                          for k in range(TOP_K)]

    @pl.when(i == 0)
    def _():
        load(0, 0).start()

    @pl.when(i > 0)
    def _():
        _wait_rows(tm, scatter(dprev_ref, 1 - slot))

    @pl.when(i + 1 < n)
    def _():
        load(i + 1, 1 - slot).start()

    load(i, slot).wait()
    _start_rows(tm, scatter(dcur_ref, slot))

    @pl.when(i == n - 1)
    def _():
        _wait_rows(tm, scatter(dcur_ref, slot))


def _dispatch(dest_flat, xn, xb, tm, rows):
    n = xn.shape[0] // ROW_TILE
    any_spec = pl.BlockSpec(memory_space=pl.ANY)
    dest_spec = lambda off: pl.BlockSpec((DEST_ROWS * tm,), lambda i: (jnp.maximum(i + off, 0),),
                                         memory_space=pltpu.SMEM)
    in_specs = [dest_spec(-1), dest_spec(0), any_spec]
    args = [dest_flat, dest_flat, xn]
    aliases = {}
    if xb is not None:
        in_specs.append(any_spec)
        args.append(xb)
        aliases = {3: 0}
    return pl.pallas_call(
        functools.partial(_dispatch_kernel, tm=tm),
        grid=(n // tm,),
        in_specs=in_specs,
        out_specs=any_spec,
        out_shape=jax.ShapeDtypeStruct((rows * ROW_TILE, LANES), F32),
        scratch_shapes=[pltpu.VMEM((2, tm * ROW_TILE, LANES), F32), pltpu.SemaphoreType.DMA((2,)),
                        pltpu.SemaphoreType.DMA((2,))],
        input_output_aliases=aliases,
        compiler_params=_cparams(1),
        name="dispatch",
    )(*args)


def _moe_kernel(be_ref, br_ref, nx_ref, sl_ref, nu_ref, xb_ref, wgu_hbm, bgu_ref, wd_hbm, bd_ref, y_ref,
                wgu_f32, wd_f32, wgu_bf, wd_bf, sems):
    i = pl.program_id(0)
    e = be_ref[i]
    slot = sl_ref[i]
    changed = jnp.logical_or(i == 0, e != be_ref[jnp.maximum(i - 1, 0)])

    def fetch(expert, s):
        return (pltpu.make_async_copy(wgu_hbm.at[expert], wgu_f32.at[s], sems.at[0, s]),
                pltpu.make_async_copy(wd_hbm.at[expert], wd_f32.at[s], sems.at[1, s]))

    @pl.when(i == 0)
    def _():
        for c in fetch(e, slot):
            c.start()

    @pl.when(changed)
    def _():
        @pl.when(nx_ref[i] >= 0)
        def _():
            for c in fetch(nx_ref[i], 1 - slot):
                c.start(priority=1)

        for c in fetch(e, slot):
            c.wait()
        wgu_bf[...] = wgu_f32[slot].astype(BF16)
        wd_bf[...] = wd_f32[slot].astype(BF16)

    @pl.when(i < nu_ref[0])
    def _():
        xb = _load_row_tiles(xb_ref, EXPERT_ROWS).astype(BF16)
        gu = jnp.dot(xb, wgu_bf[...], preferred_element_type=F32) + bgu_ref[...]
        g = jnp.minimum(gu[:, :D_EXPERT], SWIGLU_LIMIT)
        u = jnp.clip(gu[:, D_EXPERT:], -SWIGLU_LIMIT, SWIGLU_LIMIT)
        act = (u + 1.0) * (g * jax.nn.sigmoid(SWIGLU_ALPHA * g))
        _store_row_tiles(y_ref, jnp.dot(act.astype(BF16), wd_bf[...], preferred_element_type=F32) + bd_ref[...])


def _moe(sched, xb, wgu, bgu, wd, bd):
    eb = EXPERT_ROWS
    block_e = sched[0]
    any_spec = pl.BlockSpec(memory_space=pl.ANY)
    grid_spec = pltpu.PrefetchScalarGridSpec(
        num_scalar_prefetch=len(sched),
        grid=(block_e.shape[0],),
        in_specs=[
            pl.BlockSpec((eb * ROW_TILE, LANES), lambda i, be, br, *_: (br[i], 0)),
            any_spec,
            pl.BlockSpec((None, 1, 2 * D_EXPERT), lambda i, be, *_: (be[i], 0, 0)),
            any_spec,
            pl.BlockSpec((None, 1, D_MODEL), lambda i, be, *_: (be[i], 0, 0)),
        ],
        out_specs=pl.BlockSpec((eb * ROW_TILE, LANES), lambda i, be, br, *_: (br[i], 0)),
        scratch_shapes=[pltpu.VMEM((2, D_MODEL, 2 * D_EXPERT), F32), pltpu.VMEM((2, D_EXPERT, D_MODEL), F32),
                        pltpu.VMEM((D_MODEL, 2 * D_EXPERT), BF16), pltpu.VMEM((D_EXPERT, D_MODEL), BF16),
                        pltpu.SemaphoreType.DMA((2, 2))],
    )
    return pl.pallas_call(
        _moe_kernel,
        grid_spec=grid_spec,
        out_shape=jax.ShapeDtypeStruct(xb.shape, F32),
        compiler_params=_cparams(1),
        name="moe",
    )(*sched, xb, wgu, bgu, wd, bd)


def _expert_blocks(counts, cap, n_assign):
    eb = EXPERT_ROWS
    n_grid = -(-n_assign // eb) + N_EXPERTS
    nblk = (counts + eb - 1) // eb
    blk_end = jnp.cumsum(nblk)
    n_used = blk_end[-1:].astype(jnp.int32)
    i = jnp.minimum(jnp.arange(n_grid, dtype=jnp.int32), n_used[0] - 1)
    block_e = jnp.sum(blk_end[None, :] <= i[:, None], axis=1).astype(jnp.int32)
    ids = jnp.arange(N_EXPERTS, dtype=jnp.int32)
    used = nblk > 0
    later_used = jnp.logical_and(used[None, :], ids[None, :] > ids[:, None])
    next_e = jnp.min(jnp.where(later_used, ids[None, :], N_EXPERTS), axis=1)
    next_e = jnp.where(next_e < N_EXPERTS, next_e, -1).astype(jnp.int32)
    slot_e = ((jnp.cumsum(used.astype(jnp.int32)) - 1) % 2).astype(jnp.int32)
    mine = block_e[:, None] == ids[None, :]
    pick = lambda per_expert: jnp.sum(jnp.where(mine, per_expert[None, :], 0), axis=1).astype(jnp.int32)
    block_row = block_e * (cap // eb) + i - pick(blk_end - nblk)
    return (block_e, block_row.astype(jnp.int32), pick(next_e), pick(slot_e), n_used)


def _ple_kernel(dcur_ref, dnext_ref, x1_ref, gate_ref, p_ref, gp_ref, wg_ref, wp_ref, yb_ref, y_ref, ybuf, sems):
    tm = x1_ref.shape[0]
    i = pl.program_id(0)
    slot = i % 2

    def gather(dest_ref, s):
        return lambda r: [_row_tile_copy(yb_ref, dest_ref[k * tm + r], ybuf.at[s, k], r, sems.at[s])
                          for k in range(TOP_K)]

    @pl.when(i == 0)
    def _():
        _start_rows(tm, gather(dcur_ref, 0))

    @pl.when(i + 1 < pl.num_programs(0))
    def _():
        _start_rows(tm, gather(dnext_ref, 1 - slot))

    _wait_rows(tm, gather(dcur_ref, slot))

    gates = gate_ref[...]
    x2 = x1_ref[...]
    for k in range(TOP_K):
        x2 = x2 + _load_row_tiles(ybuf.at[slot, k], tm) * gates[:, k:k + 1]
    gate = jax.nn.sigmoid(jnp.dot(_rms(x2, gp_ref[...]).astype(BF16), wg_ref[...], preferred_element_type=F32))
    proj = jnp.dot(p_ref[...].astype(BF16), wp_ref[...], preferred_element_type=F32)
    y_ref[...] = x2 + gate * proj


def _ple(dest_flat, x1, gates, p2d, gp, wg, wp, yb, tm):
    n = x1.shape[0]
    row = lambda w: pl.BlockSpec((tm, w), lambda i: (i, 0))
    full = lambda shape: pl.BlockSpec(shape, lambda i: (0,) * len(shape))
    last = n // tm - 1
    dest_spec = lambda off: pl.BlockSpec((DEST_ROWS * tm,), lambda i: (jnp.minimum(i + off, last),),
                                         memory_space=pltpu.SMEM)
    return pl.pallas_call(
        _ple_kernel,
        grid=(n // tm,),
        in_specs=[dest_spec(0), dest_spec(1),
                  row(D_MODEL), row(LANES), row(D_PLE), full((1, D_MODEL)),
                  full((D_MODEL, D_MODEL)), full((D_PLE, D_MODEL)), pl.BlockSpec(memory_space=pl.ANY)],
        out_specs=row(D_MODEL),
        out_shape=jax.ShapeDtypeStruct((n, D_MODEL), F32),
        scratch_shapes=[pltpu.VMEM((2, TOP_K, tm * ROW_TILE, LANES), F32), pltpu.SemaphoreType.DMA((2,))],
        compiler_params=_cparams(1),
        name="ple",
    )(dest_flat, dest_flat, x1, gates, p2d, gp, wg, wp, yb)


def kernel(x_prompt, x_sample, state_conv, state_h, cache_kv_w128, cache_kv_w512, cache_kv_w2048, p_prompt,
           p_sample, norm_mix_g, w_in, conv_w, conv_b, w_rg_a, b_rg_a, w_rg_x, b_rg_x, lru_lambda, q_norm_g,
           k_norm_g, rel_bias, w_br_rnn, w_br_attn, w_out, norm_ffn_g, w_router, b_router, w_gate_up, b_gate_up,
           w_down, b_down, norm_ple_g, w_ple_gate, w_ple_proj):
    bp, seq, _ = x_prompt.shape
    bs = x_sample.shape[0]
    n_p = bp * seq
    li = 0
    row = lambda v: v.reshape(1, -1).astype(F32)

    w = w_in[li]
    o_q, o_ga = 2 * D_RNN, 2 * D_RNN + 3 * D_QKV
    w_perm = jnp.concatenate([w[:, :o_q], w[:, o_ga:], w[:, o_q:o_ga]], axis=1).astype(BF16)
    colgain = jnp.concatenate([jnp.ones((COL_Q,), F32), jnp.tile(q_norm_g[li], N_ATTN_HEADS),
                               jnp.tile(k_norm_g[li], N_ATTN_HEADS), jnp.ones((D_QKV,), F32)]).reshape(1, D_IN)
    wax = jnp.concatenate([w_rg_a[li], w_rg_x[li]], axis=-1).astype(BF16)
    lru = (conv_w[li].astype(F32), row(conv_b[li]), wax, row(b_rg_a[li]), row(b_rg_x[li]), row(lru_lambda[li]))
    wr, wa, wo = w_br_rnn[li].astype(BF16), w_br_attn[li].astype(BF16), w_out[li].astype(BF16)
    wrt = jnp.pad(w_router[li].astype(BF16), ((0, 0), (0, LANES - N_EXPERTS)))
    brt = jnp.pad(row(b_router[li]), ((0, 0), (0, LANES - N_EXPERTS)))
    gf = row(norm_ffn_g[li])
    biases = [_group_bias(rel_bias, gi, wnd, dil) for gi, (wnd, dil) in enumerate(GROUPS)]

    xp = x_prompt.reshape(n_p, D_MODEL)
    zp = _inproj(xp, row(norm_mix_g[li]), w_perm, colgain, tm=1024, out_dtype=BF16)
    a_p, h_p = _rglru_prompt(zp, bp, seq, RGLRU_CHUNK, *lru)
    os_p, ls_p = [], []
    for gi in range(len(GROUPS)):
        o, l = _attn_prompt(zp, _band_bias(biases[gi]), gi, bp, seq)
        os_p.append(o)
        ls_p.append(l)
    cap = -(-(n_p + bs) // EXPERT_ROWS) * EXPERT_ROWS
    x1_p, xn_p, gate_p, dest_p, cnt_p = _mix(a_p, os_p, ls_p, zp, xp, wr, wa, wo, gf, wrt, brt,
                                              jnp.zeros((1, LANES), jnp.int32), tm=MIX_TM, cap=cap)

    xs = x_sample.reshape(bs, D_MODEL)
    zs = _inproj(xs, row(norm_mix_g[li]), w_perm, colgain, tm=bs, out_dtype=F32)
    sc = state_conv[li]
    a_s, h_s = _rglru_step(zs, [sc[:, j] for j in range(CONV_WIDTH - 1)], state_h[li], *lru)
    heads = lambda c0: zs[:, c0:c0 + D_QKV].astype(F32).reshape(bs, N_ATTN_HEADS, HEAD_DIM)
    caches = [c[li].reshape(bs, QUERY_BLOCK, dil, 2, HEADS_PER_GROUP, HEAD_DIM)
              for c, (_, dil) in zip((cache_kv_w128, cache_kv_w512, cache_kv_w2048), GROUPS)]
    bias_c = jnp.stack([b[:, :0:-1].T for b in biases])[..., None]
    bias_n = jnp.stack([b[:, 0] for b in biases])[..., None]
    o_s, l_s = _attn_step(heads(COL_Q), heads(COL_K), heads(COL_V), caches, bias_c, bias_n, bb=4)
    os_s = [o_s[g].reshape(bs, D_ATTN_OUT) for g in range(len(GROUPS))]
    ls_s = [l_s[g].reshape(bs, D_ATTN_OUT) for g in range(len(GROUPS))]
    x1_s, xn_s, gate_s, dest_s, cnt_s = _mix(a_s, os_s, ls_s, zs, xs, wr, wa, wo, gf, wrt, brt, cnt_p, tm=bs, cap=cap)

    dest_p, dest_s = dest_p.reshape(-1), dest_s.reshape(-1)
    xb = _dispatch(dest_p, xn_p, None, MIX_TM, N_EXPERTS * cap)
    xb = _dispatch(dest_s, xn_s, xb, bs, N_EXPERTS * cap)
    sched = _expert_blocks(cnt_s[0, :N_EXPERTS], cap, (n_p + bs) * TOP_K)
    yb = _moe(sched, xb, w_gate_up[li], b_gate_up[li][:, None, :], w_down[li], b_down[li][:, None, :])

    gp, wg, wp = row(norm_ple_g[li]), w_ple_gate[li].astype(BF16), w_ple_proj[li].astype(BF16)
    y_p = _ple(dest_p, x1_p, gate_p, p_prompt[li].reshape(n_p, D_PLE), gp, wg, wp, yb, tm=MIX_TM)
    y_s = _ple(dest_s, x1_s, gate_s, p_sample[li].reshape(bs, D_PLE), gp, wg, wp, yb, tm=bs)

    z3 = zp.reshape(bp, seq, D_IN)
    zs3 = zs.reshape(bs, 1, D_IN)

    def new_kv(zz, gi, keep):
        t = zz.shape[1]
        c = gi * D_ATTN_OUT
        kk = zz[:, t - keep:, COL_K + c:COL_K + c + D_ATTN_OUT].astype(F32)
        vv = zz[:, t - keep:, COL_V + c:COL_V + c + D_ATTN_OUT].astype(F32)
        shape = (zz.shape[0], keep, HEADS_PER_GROUP, HEAD_DIM)
        return jnp.stack([kk.reshape(shape), vv.reshape(shape)], axis=2)[None]

    conv_p = z3[:, seq - (CONV_WIDTH - 1):, COL_XR:COL_XR + D_RNN].astype(F32)[None]
    conv_s = jnp.concatenate([sc[:, 1:], zs3[:, :, COL_XR:COL_XR + D_RNN].astype(F32)], axis=1)[None]
    outs = [y_p.reshape(bp, seq, D_MODEL), y_s.reshape(bs, 1, D_MODEL), conv_p, conv_s,
            h_p.reshape(1, bp, D_RNN), h_s.reshape(1, bs, D_RNN)]
    for gi, (wnd, _) in enumerate(GROUPS):
        outs.append(new_kv(z3, gi, min(wnd, seq)))
        outs.append(new_kv(zs3, gi, 1))
    return tuple(outs)
```
